```python
import jax
import jax.numpy as jnp
from jax import lax
import numpy as np

D_MODEL = 2048
BATCH = 2
SEQ = 4096
DEPTH = 4

GRID_W = 64
CTX_LEN = 256
CHUNK = 128
EPS = 1e-6
CONV_W = 4
DIRECTIONS = (False, True)

RET_HEADS = 8
RET_DV = D_MODEL // RET_HEADS
RET_DK = RET_DV // 2
RET_QK = RET_HEADS * RET_DK
RET_V = RET_HEADS * RET_DV
ROPE_BASE = 10000.0

SSD_INNER = D_MODEL
SSD_HEADDIM = 64
SSD_HEADS = SSD_INNER // SSD_HEADDIM
SSD_GROUPS = 4
SSD_STATE = 128
SSD_CONV_DIM = SSD_INNER + 2 * SSD_GROUPS * SSD_STATE

LRU_WIDTH = D_MODEL
LRU_BLOCKS = 16
LRU_BW = LRU_WIDTH // LRU_BLOCKS
LRU_C = 8.0

N_BRANCH = 3
BRANCH_W = D_MODEL
IN_SPLITS = (RET_QK, RET_QK, RET_V, RET_V, SSD_INNER, SSD_CONV_DIM, 2 * SSD_HEADS,
             LRU_WIDTH, LRU_WIDTH, N_BRANCH * D_MODEL)
D_IN = sum(IN_SPLITS)

D_FF = 5632
N_EXPERTS = 8
TOP_K = 2
D_FF_EXPERT = 2 * D_MODEL

kernel_name = "hybrid_ret_ssd_rglru_moe_dit"

F32 = jnp.float32


def rmsnorm(x, g):
    xf = x.astype(F32)
    y = xf * lax.rsqrt(jnp.mean(xf * xf, axis=-1, keepdims=True) + EPS)
    return (y * g.astype(F32)).astype(x.dtype)


def split_in(p):
    offs, acc = [], 0
    for s in IN_SPLITS[:-1]:
        acc += s
        offs.append(acc)
    return jnp.split(p, offs, axis=-1)


def dwconv(x, w, b):
    y = lax.conv_general_dilated(
        x, w[:, None, :].astype(x.dtype), window_strides=(1,),
        padding=[(CONV_W // 2, CONV_W - 1 - CONV_W // 2)],
        dimension_numbers=('NWC', 'WIO', 'NWC'), feature_group_count=x.shape[-1])
    return y + b.astype(x.dtype)


def to_heads(t, d):
    return t.astype(F32).reshape(t.shape[0], t.shape[1], -1, d)


def axial_rotary(n_lat):
    rows = n_lat // GRID_W
    row = jnp.repeat(jnp.arange(rows, dtype=F32), GRID_W)
    col = jnp.tile(jnp.arange(GRID_W, dtype=F32), rows)
    n_freq = RET_DK // 4
    inv = ROPE_BASE ** (-jnp.arange(n_freq, dtype=F32) / n_freq)
    ang = jnp.concatenate([row[:, None] * inv, col[:, None] * inv], axis=-1)[:, None, :]
    return jnp.cos(ang), jnp.sin(ang)


def apply_rot(t, cos, sin):
    t1, t2 = t[..., :RET_DK // 2], t[..., RET_DK // 2:]
    return jnp.concatenate([t1 * cos - t2 * sin, t1 * sin + t2 * cos], axis=-1)


def scan_ctx_then_latent(scan_fn, ctx_seq, lat_seq, consts, s0, reverse):
    order = (lambda t: jnp.flip(t, axis=1)) if reverse else (lambda t: t)
    y_ctx, s_ctx = scan_fn(*[order(t) for t in ctx_seq], *consts, s0)
    y_lat, _ = scan_fn(*[order(t) for t in lat_seq], *consts, s_ctx)
    return order(y_ctx), order(y_lat)


def retention_chunked(q, k, v, log_g, s0):
    b, n, h, dk = q.shape
    dv = v.shape[-1]
    nc = n // CHUNK
    qc = q.reshape(b, nc, CHUNK, h, dk)
    kc = k.reshape(b, nc, CHUNK, h, dk)
    vc = v.reshape(b, nc, CHUNK, h, dv)
    idx = jnp.arange(CHUNK, dtype=F32)
    diff = idx[:, None] - idx[None, :]
    decay = jnp.where(diff >= 0, jnp.exp(log_g[:, None, None] * jnp.maximum(diff, 0.0)), 0.0)
    scores = jnp.einsum('bcihd,bcjhd->bchij', qc, kc) * decay
    y_intra = jnp.einsum('bchij,bcjhe->bcihe', scores, vc)
    k_decay = jnp.exp(log_g[None, :] * (CHUNK - 1 - idx)[:, None])
    kv = jnp.einsum('bcjhd,jh,bcjhe->bchde', kc, k_decay, vc)
    chunk_decay = jnp.exp(log_g * CHUNK)[None, :, None, None]

    def step(s, kv_c):
        return chunk_decay * s + kv_c, s

    s_final, s_in = lax.scan(step, s0, jnp.moveaxis(kv, 1, 0))
    s_in = jnp.moveaxis(s_in, 0, 1)
    q_decay = jnp.exp(log_g[None, :] * (idx + 1.0)[:, None])
    y_cross = jnp.einsum('bcihd,ih,bchde->bcihe', qc, q_decay, s_in)
    return (y_intra + y_cross).reshape(b, n, h, dv), s_final


def ssd_chunked(x, a, bm, cm, s0):
    b, n, h, p = x.shape
    g, ns = bm.shape[2], bm.shape[3]
    hg = h // g
    nc = n // CHUNK
    xc = x.reshape(b, nc, CHUNK, g, hg, p)
    ac = a.reshape(b, nc, CHUNK, g, hg)
    bc = bm.reshape(b, nc, CHUNK, g, ns)
    cc = cm.reshape(b, nc, CHUNK, g, ns)
    a_cum = jnp.cumsum(ac, axis=2)
    seg = a_cum[:, :, :, None] - a_cum[:, :, None]
    causal = jnp.tril(jnp.ones((CHUNK, CHUNK), dtype=bool))[:, :, None, None]
    lmask = jnp.exp(jnp.where(causal, seg, -jnp.inf))
    cb = jnp.einsum('bclgn,bcsgn->bclsg', cc, bc)
    y_diag = jnp.einsum('bclsg,bclsgh,bcsghp->bclghp', cb, lmask, xc)
    decay_to_end = jnp.exp(a_cum[:, :, -1:] - a_cum)
    states = jnp.einsum('bclgn,bclgh,bclghp->bcghpn', bc, decay_to_end, xc)
    chunk_decay = jnp.exp(a_cum[:, :, -1])

    def step(s, inp):
        st, dec = inp
        return dec[..., None, None] * s + st, s

    s_fin, s_in = lax.scan(step, s0.reshape(b, g, hg, p, ns),
                           (jnp.moveaxis(states, 1, 0), jnp.moveaxis(chunk_decay, 1, 0)))
    s_in = jnp.moveaxis(s_in, 0, 1)
    y_off = jnp.einsum('bclgn,bcghpn,bclgh->bclghp', cc, s_in, jnp.exp(a_cum))
    return (y_diag + y_off).reshape(b, n, h, p), s_fin.reshape(b, h, p, ns)


def linear_scan(a, bx, h0):
    bx = bx.at[:, 0].add(a[:, 0] * h0)

    def combine(l, r):
        al, bl = l
        ar, br = r
        return al * ar, ar * bl + br

    _, h = lax.associative_scan(combine, (a, bx), axis=1)
    return h, h[:, -1]


def ssd_prepare(xbc, dt_raw, conv_w, conv_b, dt_bias):
    b, n, _ = xbc.shape
    xbc = jax.nn.silu(dwconv(xbc, conv_w, conv_b).astype(F32))
    xs, bm, cm = jnp.split(xbc, [SSD_INNER, SSD_INNER + SSD_GROUPS * SSD_STATE], axis=-1)
    dt = jax.nn.softplus(dt_raw.astype(F32).reshape(b, n, 2, SSD_HEADS) + dt_bias.astype(F32))
    return (xs.reshape(b, n, SSD_HEADS, SSD_HEADDIM), bm.reshape(b, n, SSD_GROUPS, SSD_STATE),
            cm.reshape(b, n, SSD_GROUPS, SSD_STATE), dt)


def lru_gates(xr, w_gates, b_gates, lam):
    b, n, _ = xr.shape
    xb = xr.reshape(b, n, LRU_BLOCKS, LRU_BW)
    gates = jnp.einsum('bnkd,gkde->gbnke', xb, w_gates.astype(F32)).reshape(2, b, n, LRU_WIDTH)
    gates = jax.nn.sigmoid(gates + b_gates.astype(F32)[:, None, None, :])
    r, i = gates[0], gates[1]
    log_a = -LRU_C * r * jax.nn.softplus(-lam.astype(F32))
    a = jnp.exp(log_a)
    mult = jnp.sqrt(jnp.maximum(-jnp.expm1(2.0 * log_a), 0.0))
    return a, mult * (i * xr)


def head_groupnorm(y, g):
    b, n = y.shape[0], y.shape[1]
    mu = jnp.mean(y, axis=-1, keepdims=True)
    var = jnp.mean(jnp.square(y - mu), axis=-1, keepdims=True)
    return ((y - mu) * lax.rsqrt(var + EPS)).reshape(b, n, -1) * g.astype(F32)


def branch_merge(u_ret, u_ssd, u_lru, merge_raw, w_branch, w_out):
    u = jnp.stack([u_ret, u_ssd, u_lru], axis=2)
    p = jnp.einsum('bnkw,kwd->bnkd', u, w_branch)
    gates = jax.nn.sigmoid(merge_raw.astype(F32)).astype(p.dtype).reshape(p.shape)
    return jnp.sum(gates * p, axis=2) @ w_out


def mixer_forward(hl, hc, cos_l, sin_l, w_in, ret_decay, ret_gn_g, ssd_conv_w, ssd_conv_b,
                  ssd_dt_bias, ssd_a_log, ssd_d, ssd_norm_g, lru_conv_w, lru_conv_b,
                  lru_gate_w, lru_gate_b, lru_lambda, w_branch, w_out, ctx_out):
    dtype = hl.dtype
    b = hl.shape[0]
    ql, kl, vl, gl, zl, xbcl, dtl, gatel, recl, mergel = split_in(hl @ w_in)
    qc, kc, vc, gc, zc, xbcc, dtc, gatec, recc, mergec = split_in(hc @ w_in)

    scale = RET_DK ** -0.5
    lat_seq = (apply_rot(to_heads(ql, RET_DK), cos_l, sin_l),
               apply_rot(to_heads(kl, RET_DK), cos_l, sin_l) * scale,
               to_heads(vl, RET_DV))
    ctx_seq = (to_heads(qc, RET_DK), to_heads(kc, RET_DK) * scale, to_heads(vc, RET_DV))
    s0 = jnp.zeros((b, RET_HEADS, RET_DK, RET_DV), F32)
    ret_l, ret_c = 0.0, 0.0
    for d, rev in enumerate(DIRECTIONS):
        log_g = jax.nn.log_sigmoid(ret_decay[d].astype(F32))
        yc, yl = scan_ctx_then_latent(retention_chunked, ctx_seq, lat_seq, (log_g,), s0, rev)
        ret_l, ret_c = ret_l + yl, ret_c + yc

    A = -jnp.exp(ssd_a_log.astype(F32))
    xs_l, b_l, c_l, dt_l = ssd_prepare(xbcl, dtl, ssd_conv_w, ssd_conv_b, ssd_dt_bias)
    xs_c, b_c, c_c, dt_c = ssd_prepare(xbcc, dtc, ssd_conv_w, ssd_conv_b, ssd_dt_bias)
    s0 = jnp.zeros((b, SSD_HEADS, SSD_HEADDIM, SSD_STATE), F32)
    d_skip = ssd_d.astype(F32)[:, None]
    ssd_l, ssd_c = d_skip * xs_l, d_skip * xs_c
    for d, rev in enumerate(DIRECTIONS):
        seq_c = (xs_c * dt_c[:, :, d, :, None], dt_c[:, :, d] * A[d], b_c, c_c)
        seq_l = (xs_l * dt_l[:, :, d, :, None], dt_l[:, :, d] * A[d], b_l, c_l)
        yc, yl = scan_ctx_then_latent(ssd_chunked, seq_c, seq_l, (), s0, rev)
        ssd_l, ssd_c = ssd_l + yl, ssd_c + yc

    xr_l = dwconv(recl, lru_conv_w, lru_conv_b).astype(F32)
    xr_c = dwconv(recc, lru_conv_w, lru_conv_b).astype(F32)
    h0 = jnp.zeros((b, LRU_WIDTH), F32)
    lru_l, lru_c = 0.0, 0.0
    for d, rev in enumerate(DIRECTIONS):
        a_l, bx_l = lru_gates(xr_l, lru_gate_w[d], lru_gate_b[d], lru_lambda[d])
        a_c, bx_c = lru_gates(xr_c, lru_gate_w[d], lru_gate_b[d], lru_lambda[d])
        yc, yl = scan_ctx_then_latent(linear_scan, (a_c, bx_c), (a_l, bx_l), (), h0, rev)
        lru_l, lru_c = lru_l + yl, lru_c + yc

    def branch_outputs(ret, g, ssd, z, lru, gate):
        n = ret.shape[1]
        u_ret = (jax.nn.silu(g.astype(F32)) * head_groupnorm(ret, ret_gn_g)).astype(dtype)
        u_ssd = rmsnorm(ssd.reshape(b, n, SSD_INNER) * jax.nn.silu(z.astype(F32)), ssd_norm_g).astype(dtype)
        u_lru = (lru * jax.nn.gelu(gate.astype(F32))).astype(dtype)
        return u_ret, u_ssd, u_lru

    out_l = branch_merge(*branch_outputs(ret_l, gl, ssd_l, zl, lru_l, gatel), mergel, w_branch, w_out)
    out_c = None
    if ctx_out:
        out_c = branch_merge(*branch_outputs(ret_c, gc, ssd_c, zc, lru_c, gatec), mergec, w_branch, w_out)
    return out_l, out_c


def swiglu(h, w1, w3, w2):
    return (jax.nn.silu(h @ w1) * (h @ w3)) @ w2


def moe_swiglu(h, router_w, router_b, w1, w3, w2):
    logits = (h @ router_w).astype(F32) + router_b.astype(F32)
    top_vals, top_idx = lax.top_k(logits, TOP_K)
    weights = jax.nn.softmax(top_vals, axis=-1)
    gate = jnp.sum(jax.nn.one_hot(top_idx, N_EXPERTS, dtype=F32) * weights[..., None], axis=-2)
    gate = gate.astype(h.dtype)
    out = jnp.zeros_like(h)
    for e in range(N_EXPERTS):
        out = out + gate[..., e:e + 1] * swiglu(h, w1[e], w3[e], w2[e])
    return out


def channel_mixer(h, i, ffn_w1, ffn_w3, ffn_w2, router_w, router_b, moe_w1, moe_w3, moe_w2):
    j = i // 2
    if i % 2 == 0:
        return swiglu(h, ffn_w1[j], ffn_w3[j], ffn_w2[j])
    return moe_swiglu(h, router_w[j], router_b[j], moe_w1[j], moe_w3[j], moe_w2[j])


def setup_inputs(seed: int = 0) -> dict:
    key = jax.random.key(seed)
    ks = iter(jax.random.split(key, 48))

    def nrm(shape, scale):
        return jax.random.normal(next(ks), shape, F32) * scale

    def uni(shape, lo, hi):
        return jax.random.uniform(next(ks), shape, F32, lo, hi)

    n_dense = (DEPTH + 1) // 2
    n_moe = DEPTH // 2
    gam = 1.0 - 2.0 ** (-5.0 - np.arange(RET_HEADS))
    ret_logit = jnp.asarray(np.log(gam / (1.0 - gam)), F32)
    dt = jnp.exp(uni((DEPTH, 2, SSD_HEADS), float(np.log(1e-3)), float(np.log(1e-1))))
    a0 = uni((DEPTH, 2, LRU_WIDTH), 0.9, 0.999) ** (1.0 / LRU_C)
    inp = {}
    inp['x'] = nrm((BATCH, SEQ, D_MODEL), 1.0)
    inp['c'] = nrm((BATCH, D_MODEL), 1.0)
    inp['ctx'] = nrm((BATCH, CTX_LEN, D_MODEL), 1.0)
    inp['c_ctx'] = nrm((D_MODEL,), 1.0)
    inp['w_mod'] = nrm((DEPTH, D_MODEL, 6 * D_MODEL), 0.5 * D_MODEL ** -0.5)
    inp['b_mod'] = nrm((DEPTH, 6 * D_MODEL), 0.02)
    inp['norm1_g'] = 1.0 + nrm((DEPTH, D_MODEL), 0.02)
    inp['norm2_g'] = 1.0 + nrm((DEPTH, D_MODEL), 0.02)
    inp['w_in'] = nrm((DEPTH, D_MODEL, D_IN), D_MODEL ** -0.5)
    inp['ret_decay'] = ret_logit + nrm((DEPTH, 2, RET_HEADS), 0.1)
    inp['ret_gn_g'] = 1.0 + nrm((DEPTH, RET_V), 0.02)
    inp['ssd_conv_w'] = nrm((DEPTH, CONV_W, SSD_CONV_DIM), CONV_W ** -0.5)
    inp['ssd_conv_b'] = nrm((DEPTH, SSD_CONV_DIM), 0.02)
    inp['ssd_dt_bias'] = dt + jnp.log(-jnp.expm1(-dt))
    inp['ssd_a_log'] = jnp.log(uni((DEPTH, 2, SSD_HEADS), 1.0, 16.0))
    inp['ssd_d'] = 1.0 + nrm((DEPTH, SSD_HEADS), 0.02)
    inp['ssd_norm_g'] = 1.0 + nrm((DEPTH, SSD_INNER), 0.02)
    inp['lru_conv_w'] = nrm((DEPTH, CONV_W, LRU_WIDTH), CONV_W ** -0.5)
    inp['lru_conv_b'] = nrm((DEPTH, LRU_WIDTH), 0.02)
    inp['lru_gate_w'] = nrm((DEPTH, 2, 2, LRU_BLOCKS, LRU_BW, LRU_BW), LRU_BW ** -0.5)
    inp['lru_gate_b'] = nrm((DEPTH, 2, 2, LRU_WIDTH), 0.02)
    inp['lru_lambda'] = jnp.log(a0) - jnp.log1p(-a0)
    inp['w_branch'] = nrm((DEPTH, N_BRANCH, BRANCH_W, D_MODEL), BRANCH_W ** -0.5)
    inp['w_out'] = nrm((DEPTH, D_MODEL, D_MODEL), D_MODEL ** -0.5)
    inp['ffn_w1'] = nrm((n_dense, D_MODEL, D_FF), D_MODEL ** -0.5)
    inp['ffn_w3'] = nrm((n_dense, D_MODEL, D_FF), D_MODEL ** -0.5)
    inp['ffn_w2'] = nrm((n_dense, D_FF, D_MODEL), D_FF ** -0.5)
    inp['router_w'] = nrm((n_moe, D_MODEL, N_EXPERTS), D_MODEL ** -0.5)
    inp['router_b'] = nrm((n_moe, N_EXPERTS), 0.01)
    inp['moe_w1'] = nrm((n_moe, N_EXPERTS, D_MODEL, D_FF_EXPERT), D_MODEL ** -0.5)
    inp['moe_w3'] = nrm((n_moe, N_EXPERTS, D_MODEL, D_FF_EXPERT), D_MODEL ** -0.5)
    inp['moe_w2'] = nrm((n_moe, N_EXPERTS, D_FF_EXPERT, D_MODEL), D_FF_EXPERT ** -0.5)
    inp['final_g'] = 1.0 + nrm((D_MODEL,), 0.02)
    return inp


def reference(x, c, ctx, c_ctx, w_mod, b_mod, norm1_g, norm2_g, w_in, ret_decay, ret_gn_g,
              ssd_conv_w, ssd_conv_b, ssd_dt_bias, ssd_a_log, ssd_d, ssd_norm_g,
              lru_conv_w, lru_conv_b, lru_gate_w, lru_gate_b, lru_lambda, w_branch, w_out,
              ffn_w1, ffn_w3, ffn_w2, router_w, router_b, moe_w1, moe_w3, moe_w2, final_g):
    cos_l, sin_l = axial_rotary(x.shape[1])
    cond_l = jax.nn.silu(c)[:, None, :]
    cond_c = jax.nn.silu(c_ctx)[None, None, :]
    xl, xc = x, ctx
    for i in range(DEPTH):
        last = i == DEPTH - 1
        sh1_l, sc1_l, g1_l, sh2_l, sc2_l, g2_l = jnp.split(cond_l @ w_mod[i] + b_mod[i], 6, axis=-1)
        sh1_c, sc1_c, g1_c, sh2_c, sc2_c, g2_c = jnp.split(cond_c @ w_mod[i] + b_mod[i], 6, axis=-1)
        hl = rmsnorm(xl, norm1_g[i]) * (1.0 + sc1_l) + sh1_l
        hc = rmsnorm(xc, norm1_g[i]) * (1.0 + sc1_c) + sh1_c
        yl, yc = mixer_forward(hl, hc, cos_l, sin_l, w_in[i], ret_decay[i], ret_gn_g[i],
                               ssd_conv_w[i], ssd_conv_b[i], ssd_dt_bias[i], ssd_a_log[i], ssd_d[i],
                               ssd_norm_g[i], lru_conv_w[i], lru_conv_b[i], lru_gate_w[i],
                               lru_gate_b[i], lru_lambda[i], w_branch[i], w_out[i], not last)
        xl = xl + g1_l * yl
        hl = rmsnorm(xl, norm2_g[i]) * (1.0 + sc2_l) + sh2_l
        xl = xl + g2_l * channel_mixer(hl, i, ffn_w1, ffn_w3, ffn_w2, router_w, router_b,
                                       moe_w1, moe_w3, moe_w2)
        if not last:
            xc = xc + g1_c * yc
            hc = rmsnorm(xc, norm2_g[i]) * (1.0 + sc2_c) + sh2_c
            xc = xc + g2_c * channel_mixer(hc, i, ffn_w1, ffn_w3, ffn_w2, router_w, router_b,
                                           moe_w1, moe_w3, moe_w2)
    return rmsnorm(xl, final_g)
```

```python
import functools

import jax
import jax.numpy as jnp
from jax import lax
from jax.experimental import pallas as pl
from jax.experimental.pallas import tpu as pltpu

F32 = jnp.float32
BF16 = jnp.bfloat16

EPS = 1e-6
CHUNK = 128
CONV_W = 4
RET_HEADS = 8
RET_DK = 128
RET_DV = 256
SSD_HEADS = 32
SSD_HEADDIM = 64
SSD_GROUPS = 4
SSD_STATE = 128
LRU_BW = 128
LRU_C = 8.0
N_EXPERTS = 8
ROPE_BASE = 10000.0
GRID_W = 64

MIB = 1024 * 1024

_NT = (((1,), (1,)), ((), ()))
_TN = (((0,), (0,)), ((), ()))


def _params(n_axes, vmem_mib):
    return pltpu.CompilerParams(dimension_semantics=("arbitrary",) * n_axes,
                                vmem_limit_bytes=int(vmem_mib * MIB))


def _silu(x):
    return x * jax.nn.sigmoid(x)


def _dot(a, b):
    return jnp.dot(a, b, preferred_element_type=F32)


def _mod_kernel(c_ref, w_ref, b_ref, o_ref):
    x = _silu(c_ref[...]).astype(BF16)
    o_ref[...] = _dot(x, w_ref[...].astype(BF16)) + b_ref[...]


def modulation(c8, w_mod, b_mod):
    depth, d, n6 = w_mod.shape
    tn = 1536
    return pl.pallas_call(
        _mod_kernel,
        grid=(depth, n6 // tn),
        in_specs=[pl.BlockSpec((8, d), lambda l, j: (0, 0)),
                  pl.BlockSpec((None, d, tn), lambda l, j: (l, 0, j)),
                  pl.BlockSpec((None, 1, tn), lambda l, j: (l, 0, j))],
        out_specs=pl.BlockSpec((None, 8, tn), lambda l, j: (l, 0, j)),
        out_shape=jax.ShapeDtypeStruct((depth, 8, n6), F32),
        compiler_params=_params(2, 48),
        name="modulation",
    )(c8, w_mod, b_mod.reshape(depth, 1, n6))


def _resnorm_kernel(*refs, has_y, has_mod):
    refs = list(refs)
    x_ref = refs.pop(0)
    if has_y:
        y_ref, gate_ref = refs.pop(0), refs.pop(0)
    g_ref = refs.pop(0)
    if has_mod:
        sc_ref, sh_ref = refs.pop(0), refs.pop(0)
    x = x_ref[...].astype(F32)
    if has_y:
        xo_ref = refs.pop(0)
        x = x + gate_ref[...] * y_ref[...].astype(F32)
        xo_ref[...] = x
    h_ref = refs.pop(0)
    ms = jnp.mean(x * x, axis=-1, keepdims=True)
    yn = x * lax.rsqrt(ms + EPS) * g_ref[...]
    if has_mod:
        yn = yn * (1.0 + sc_ref[...]) + sh_ref[...]
    h_ref[...] = yn.astype(h_ref.dtype)


def resnorm(x, y, mods3, norm_g, *, seg_fn, gate, sc, sh, out_dtype, tm):
    m, d = x.shape
    has_y = y is not None
    has_mod = sc is not None

    def mod_spec(layer_chunk):
        layer, chunk = layer_chunk
        return pl.BlockSpec((None, 1, d), lambda i: ((layer * 8 + seg_fn(i)) * 6 + chunk, 0, 0))

    row = pl.BlockSpec((tm, d), lambda i: (i, 0))
    in_specs, args = [row], [x]
    if has_y:
        in_specs += [row, mod_spec(gate)]
        args += [y, mods3]
    in_specs.append(pl.BlockSpec((1, d), lambda i: (0, 0)))
    args.append(norm_g)
    if has_mod:
        in_specs += [mod_spec(sc), mod_spec(sh)]
        args += [mods3, mods3]
    out_specs, out_shape = [], []
    if has_y:
        out_specs.append(row)
        out_shape.append(jax.ShapeDtypeStruct((m, d), F32))
    out_specs.append(row)
    out_shape.append(jax.ShapeDtypeStruct((m, d), out_dtype))
    res = pl.pallas_call(
        functools.partial(_resnorm_kernel, has_y=has_y, has_mod=has_mod),
        grid=(m // tm,), in_specs=in_specs, out_specs=out_specs, out_shape=out_shape,
        compiler_params=_params(1, 48), name="resnorm",
    )(*args)
    return (res[0], res[1]) if has_y else (x, res[0])


def _mm_kernel(x_ref, w_ref, o_ref, *scratch):
    if scratch:
        wb = scratch[0]

        @pl.when(pl.program_id(1) == 0)
        def _():
            wb[...] = w_ref[...].astype(BF16)

        w = wb[...]
    else:
        w = w_ref[...]
    o_ref[...] = _dot(x_ref[...], w).astype(o_ref.dtype)


def matmul(x, w, prefix, *, col_block0, n_cols, tn, tm, out_dtype, m_rows=None):
    m, k = x.shape
    m = m_rows or m
    cast = w.dtype != BF16
    lead = (None,) * len(prefix)
    return pl.pallas_call(
        _mm_kernel,
        grid=(n_cols // tn, m // tm),
        in_specs=[pl.BlockSpec((tm, k), lambda j, i: (i, 0)),
                  pl.BlockSpec(lead + (k, tn), lambda j, i: tuple(prefix) + (0, col_block0 + j))],
        out_specs=pl.BlockSpec((tm, tn), lambda j, i: (i, j)),
        out_shape=jax.ShapeDtypeStruct((m, n_cols), out_dtype),
        scratch_shapes=[pltpu.VMEM((k, tn), BF16)] if cast else [],
        compiler_params=_params(2, 56), name="matmul",
    )(x, w)


def _conv_kernel(x_ref, w_ref, b_ref, o_ref, scr, *, n, act, rc):
    tc = x_ref.shape[1]
    scr[0:8, :] = jnp.zeros((8, tc), F32)
    scr[8 + n:16 + n, :] = jnp.zeros((8, tc), F32)

    def copy_in(c, carry):
        r0 = pl.multiple_of(c * rc, rc)
        scr[pl.ds(8 + r0, rc), :] = x_ref[pl.ds(r0, rc), :].astype(F32)
        return carry

    lax.fori_loop(0, n // rc, copy_in, 0)
    w = w_ref[...]
    b = b_ref[...]

    def conv(c, carry):
        r0 = pl.multiple_of(c * rc, rc)
        win = scr[pl.ds(r0, rc + 16), :]
        acc = b + w[2:3] * win[8:8 + rc]
        for j in (0, 1, 3):
            rolled = pltpu.roll(win, (2 - j) % (rc + 16), 0)
            acc = acc + w[j:j + 1] * rolled[8:8 + rc]
        if act:
            acc = _silu(acc)
        o_ref[pl.ds(r0, rc), :] = acc.astype(o_ref.dtype)
        return carry

    lax.fori_loop(0, n // rc, conv, 0)


def dwconv(p, conv_w, conv_b, layer, *, n, nb, row_block0, col0, n_cols, act, out_dtype):
    tc = 512
    rc = 64
    depth = conv_w.shape[0]
    return pl.pallas_call(
        functools.partial(_conv_kernel, n=n, act=act, rc=rc),
        grid=(nb, n_cols // tc),
        in_specs=[pl.BlockSpec((n, tc), lambda b, j: (row_block0 + b, col0 // tc + j)),
                  pl.BlockSpec((None, CONV_W, tc), lambda b, j: (layer, 0, j)),
                  pl.BlockSpec((None, 1, tc), lambda b, j: (layer, 0, j))],
        out_specs=pl.BlockSpec((n, tc), lambda b, j: (b, j)),
        out_shape=jax.ShapeDtypeStruct((nb * n, n_cols), out_dtype),
        scratch_shapes=[pltpu.VMEM((n + 16, tc), F32)],
        compiler_params=_params(2, 48), name="dwconv",
    )(p, conv_w, conv_b.reshape(depth, 1, -1))


def _ret_kernel(lg_ref, ql, kl, vl, gl, qc, kc, vc, gc, cos_ref, sin_ref, gn_ref,
                ul_ref, uc_ref, qs, ks, acc, *, n, c):
    h = pl.program_id(1)
    lgf = lg_ref[0, h]
    lgb = lg_ref[1, h]
    ii = lax.broadcasted_iota(jnp.int32, (CHUNK, CHUNK), 0).astype(F32)
    jj = lax.broadcasted_iota(jnp.int32, (CHUNK, CHUNK), 1).astype(F32)
    diff = ii - jj
    dsum = (jnp.where(diff >= 0, jnp.exp(lgf * jnp.maximum(diff, 0.0)), 0.0)
            + jnp.where(diff <= 0, jnp.exp(lgb * jnp.maximum(-diff, 0.0)), 0.0))
    col = lax.broadcasted_iota(jnp.int32, (CHUNK, 1), 0).astype(F32)
    kdec_f = jnp.exp(lgf * (CHUNK - 1.0 - col))
    kdec_b = jnp.exp(lgb * col)
    qdec_f = jnp.exp(lgf * (col + 1.0))
    qdec_b = jnp.exp(lgb * (CHUNK - col))
    one = jnp.ones((1, 1), F32)
    cd_f = jnp.exp(one * (lgf * CHUNK))
    cd_b = jnp.exp(one * (lgb * CHUNK))
    scale = RET_DK ** -0.5

    def fwd_chunk(r0, q, k, v, s_in):
        qs[pl.ds(r0, CHUNK), :] = q
        ks[pl.ds(r0, CHUNK), :] = k
        s = lax.dot_general(q, k, _NT, preferred_element_type=F32)
        y = _dot((s * dsum).astype(BF16), v)
        y = y + _dot((q.astype(F32) * qdec_f).astype(BF16), s_in.astype(BF16))
        kv = lax.dot_general((k.astype(F32) * kdec_f).astype(BF16), v, _TN, preferred_element_type=F32)
        acc[pl.ds(r0, CHUNK), :] = y
        return cd_f * s_in + kv

    def bwd_chunk(r0, v, s_in):
        q = qs[pl.ds(r0, CHUNK), :]
        k = ks[pl.ds(r0, CHUNK), :]
        y = _dot((q.astype(F32) * qdec_b).astype(BF16), s_in.astype(BF16))
        acc[pl.ds(r0, CHUNK), :] = acc[pl.ds(r0, CHUNK), :] + y
        kv = lax.dot_general((k.astype(F32) * kdec_b).astype(BF16), v, _TN, preferred_element_type=F32)
        return cd_b * s_in + kv

    def rot(t_ref, r):
        t = t_ref[pl.ds(r, CHUNK), :].astype(F32)
        return t * cos_ref[pl.ds(r, CHUNK), :] + pltpu.roll(t, RET_DK // 2, 1) * sin_ref[pl.ds(r, CHUNK), :]

    s_f = jnp.zeros((RET_DK, RET_DV), F32)
    for ci in range(c // CHUNK):
        r = ci * CHUNK
        q = qc[r:r + CHUNK, :]
        k = (kc[r:r + CHUNK, :].astype(F32) * scale).astype(BF16)
        s_f = fwd_chunk(r, q, k, vc[r:r + CHUNK, :], s_f)

    def fwd_lat(i, s_in):
        r = pl.multiple_of(i * CHUNK, CHUNK)
        q = rot(ql, r).astype(BF16)
        k = (rot(kl, r) * scale).astype(BF16)
        return fwd_chunk(c + r, q, k, vl[pl.ds(r, CHUNK), :], s_in)

    lax.fori_loop(0, n // CHUNK, fwd_lat, s_f)

    s_b = jnp.zeros((RET_DK, RET_DV), F32)
    for ci in reversed(range(c // CHUNK)):
        r = ci * CHUNK
        s_b = bwd_chunk(r, vc[r:r + CHUNK, :], s_b)

    def bwd_lat(i, s_in):
        r = pl.multiple_of(n - CHUNK - i * CHUNK, CHUNK)
        return bwd_chunk(c + r, vl[pl.ds(r, CHUNK), :], s_in)

    lax.fori_loop(0, n // CHUNK, bwd_lat, s_b)

    gn = gn_ref[...]

    def finish(y, g):
        mu = jnp.mean(y, axis=-1, keepdims=True)
        yc = y - mu
        var = jnp.mean(yc * yc, axis=-1, keepdims=True)
        return (_silu(g.astype(F32)) * (yc * lax.rsqrt(var + EPS) * gn)).astype(BF16)

    for ci in range(c // CHUNK):
        r = ci * CHUNK
        uc_ref[r:r + CHUNK, :] = finish(acc[r:r + CHUNK, :], gc[r:r + CHUNK, :])

    def fin_lat(i, carry):
        r = pl.multiple_of(i * CHUNK, CHUNK)
        ul_ref[pl.ds(r, CHUNK), :] = finish(acc[pl.ds(c + r, CHUNK), :], gl[pl.ds(r, CHUNK), :])
        return carry

    lax.fori_loop(0, n // CHUNK, fin_lat, 0)


def retention(pa, cos2, sin2, log_g, gn_g, layer, *, nb, n, c):
    rb = (nb * n) // c
    lat = lambda w, off: pl.BlockSpec((n, w), lambda b, h: (b, off + h))
    ctx = lambda w, off: pl.BlockSpec((c, w), lambda b, h: (rb + b, off + h))
    full = lambda: pl.BlockSpec((n, RET_DK), lambda b, h: (0, 0))
    depth = gn_g.shape[0]
    return pl.pallas_call(
        functools.partial(_ret_kernel, n=n, c=c),
        grid=(nb, RET_HEADS),
        in_specs=[pl.BlockSpec(memory_space=pltpu.SMEM),
                  lat(RET_DK, 0), lat(RET_DK, 8), lat(RET_DV, 8), lat(RET_DV, 16),
                  ctx(RET_DK, 0), ctx(RET_DK, 8), ctx(RET_DV, 8), ctx(RET_DV, 16),
                  full(), full(),
                  pl.BlockSpec((None, 1, RET_DV), lambda b, h: (layer, 0, h))],
        out_specs=[pl.BlockSpec((n, RET_DV), lambda b, h: (b, h)),
                   pl.BlockSpec((c, RET_DV), lambda b, h: (b, h))],
        out_shape=[jax.ShapeDtypeStruct((nb * n, RET_HEADS * RET_DV), BF16),
                   jax.ShapeDtypeStruct((nb * c, RET_HEADS * RET_DV), BF16)],
        scratch_shapes=[pltpu.VMEM((c + n, RET_DK), BF16), pltpu.VMEM((c + n, RET_DK), BF16),
                        pltpu.VMEM((c + n, RET_DV), F32)],
        compiler_params=_params(2, 48), name="retention",
    )(log_g, pa, pa, pa, pa, pa, pa, pa, pa, cos2, sin2, gn_g.reshape(depth, 1, -1))


def _split3(x):
    hi = x.astype(BF16)
    r = x - hi.astype(F32)
    mid = r.astype(BF16)
    lo = (r - mid.astype(F32)).astype(BF16)
    return hi, mid, lo


def _dot_split_l(x, m, terms):
    parts = _split3(x)[:terms]
    out = _dot(parts[0], m)
    for p in parts[1:]:
        out = out + _dot(p, m)
    return out


def _dot_split_r(m, x, terms):
    parts = _split3(x)[:terms]
    out = _dot(m, parts[0])
    for p in parts[1:]:
        out = out + _dot(m, p)
    return out


def _ssd_kernel(xl, bl, cl, dtl, zl, xc, bc, cc, dtc, zc, dtb_ref, an_ref, dsk_ref,
                yl_ref, yc_ref, acc, s_f, s_b, pt_f, pt_b, *, n, c):
    g = pl.program_id(1)
    L = CHUNK
    hg = SSD_HEADS // SSD_GROUPS
    gw = hg * SSD_HEADDIM
    ii = lax.broadcasted_iota(jnp.int32, (L, L), 0)
    jj = lax.broadcasted_iota(jnp.int32, (L, L), 1)
    causal = ii >= jj
    anti = ii <= jj
    tril = causal.astype(BF16)
    triu = anti.astype(BF16)
    kk = lax.broadcasted_iota(jnp.int32, (L, gw), 0)
    mm = lax.shift_right_logical(lax.broadcasted_iota(jnp.int32, (L, gw), 1), 6)
    e_f = (kk == hg * g + mm).astype(BF16)
    e_b = (kk == SSD_HEADS + hg * g + mm).astype(BF16)
    lane2 = lax.broadcasted_iota(jnp.int32, (L, 2 * SSD_HEADDIM), 1)
    left = lane2 < SSD_HEADDIM
    dtb = dtb_ref[...]
    an = an_ref[...]
    dsk = dsk_ref[...]

    def prep_dir(dt, a, tri, e):
        psm = _dot_split_r(tri, a, 3)
        pexp = _dot_split_l(psm, e, 3)
        dtexp = _dot_split_l(dt, e, 2)
        return psm, pexp, dtexp

    def fwd_chunk(r0, x, bm, cm, dt_raw):
        xf = x.astype(F32)
        dt = jax.nn.softplus(dt_raw + dtb)
        a = dt * an
        psm_f, pexp_f, dtexp_f = prep_dir(dt, a, tril, e_f)
        psm_b, pexp_b, dtexp_b = prep_dir(dt, a, triu, e_b)
        pt_f[...] = psm_f.T
        pt_b[...] = psm_b.T
        xdt_f = xf * dtexp_f
        xdt_b = xf * dtexp_b
        xcat = jnp.concatenate([xdt_f.astype(BF16), xdt_b.astype(BF16)], axis=0)
        cb = lax.dot_general(cm, bm, _NT, preferred_element_type=F32)
        for jp in range(hg // 2):
            lo = jp * 2 * SSD_HEADDIM
            xpair = xcat[:, lo:lo + 2 * SSD_HEADDIM]
            ys = []
            for j in (2 * jp, 2 * jp + 1):
                col = j * SSD_HEADDIM
                seg_f = pexp_f[:, col:col + 1] - pt_f[pl.ds(hg * g + j, 1), :]
                seg_b = pexp_b[:, col:col + 1] - pt_b[pl.ds(SSD_HEADS + hg * g + j, 1), :]
                m_f = cb * jnp.exp(jnp.where(causal, seg_f, -jnp.inf))
                m_b = cb * jnp.exp(jnp.where(anti, seg_b, -jnp.inf))
                mcat = jnp.concatenate([m_f.astype(BF16), m_b.astype(BF16)], axis=1)
                ys.append(_dot(mcat, xpair))
            acc[pl.ds(r0, L), lo:lo + 2 * SSD_HEADDIM] = jnp.where(left, ys[0], ys[1])
        s_in = s_f[...]
        tot = pexp_f[L - 1:L, :]
        y_off = _dot(cm, s_in.astype(BF16)) * jnp.exp(pexp_f)
        st = lax.dot_general(bm, (xdt_f * jnp.exp(tot - pexp_f)).astype(BF16), _TN, preferred_element_type=F32)
        s_f[...] = jnp.exp(tot) * s_in + st
        acc[pl.ds(r0, L), :] = acc[pl.ds(r0, L), :] + y_off + dsk * xf

    def bwd_chunk(r0, x, bm, cm, dt_raw):
        xf = x.astype(F32)
        dt = jax.nn.softplus(dt_raw + dtb)
        a = dt * an
        _, pexp_b, dtexp_b = prep_dir(dt, a, triu, e_b)
        xdt_b = xf * dtexp_b
        s_in = s_b[...]
        tot = pexp_b[0:1, :]
        y_off = _dot(cm, s_in.astype(BF16)) * jnp.exp(pexp_b)
        st = lax.dot_general(bm, (xdt_b * jnp.exp(tot - pexp_b)).astype(BF16), _TN, preferred_element_type=F32)
        s_b[...] = jnp.exp(tot) * s_in + st
        acc[pl.ds(r0, L), :] = acc[pl.ds(r0, L), :] + y_off

    s_f[...] = jnp.zeros_like(s_f)
    s_b[...] = jnp.zeros_like(s_b)
    nc_ctx = c // L
    for ci in range(nc_ctx):
        r = ci * L
        fwd_chunk(r, xc[r:r + L, :], bc[r:r + L, :], cc[r:r + L, :], dtc[r:r + L, :])

    def fwd_lat(i, carry):
        r = pl.multiple_of(i * L, L)
        fwd_chunk(c + r, xl[pl.ds(r, L), :], bl[pl.ds(r, L), :], cl[pl.ds(r, L), :], dtl[pl.ds(r, L), :])
        return carry

    lax.fori_loop(0, n // L, fwd_lat, 0)

    for ci in reversed(range(nc_ctx)):
        r = ci * L
        bwd_chunk(r, xc[r:r + L, :], bc[r:r + L, :], cc[r:r + L, :], dtc[r:r + L, :])

    def bwd_lat(i, carry):
        r = pl.multiple_of(n - L - i * L, L)
        bwd_chunk(c + r, xl[pl.ds(r, L), :], bl[pl.ds(r, L), :], cl[pl.ds(r, L), :], dtl[pl.ds(r, L), :])
        return carry

    lax.fori_loop(0, n // L, bwd_lat, 0)

    for ci in range(nc_ctx):
        r = ci * L
        yc_ref[r:r + L, :] = (acc[r:r + L, :] * _silu(zc[r:r + L, :].astype(F32))).astype(BF16)

    def fin_lat(i, carry):
        r = pl.multiple_of(i * L, L)
        yl_ref[pl.ds(r, L), :] = (acc[pl.ds(c + r, L), :] * _silu(zl[pl.ds(r, L), :].astype(F32))).astype(BF16)
        return carry

    lax.fori_loop(0, n // L, fin_lat, 0)


def ssd(xbc_l, xbc_c, pdt, pa, dtb, aneg, dskip, layer, *, nb, n, c):
    hg = SSD_HEADS // SSD_GROUPS
    gw = hg * SSD_HEADDIM
    rb = (nb * n) // c
    inner = SSD_HEADS * SSD_HEADDIM
    xoff = 0
    boff = inner // SSD_STATE
    coff = boff + SSD_GROUPS
    zoff = 6144 // gw
    lat = lambda w, off: pl.BlockSpec((n, w), lambda b, g: (b, off + g))
    ctx = lambda w, off: pl.BlockSpec((c, w), lambda b, g: (b, off + g))
    ctxp = lambda w, off: pl.BlockSpec((c, w), lambda b, g: (rb + b, off + g))
    vec = lambda w: pl.BlockSpec((None, 1, w), lambda b, g: (layer, 0, 0))
    return pl.pallas_call(
        functools.partial(_ssd_kernel, n=n, c=c),
        grid=(nb, SSD_GROUPS),
        in_specs=[lat(gw, xoff), lat(SSD_STATE, boff), lat(SSD_STATE, coff),
                  pl.BlockSpec((n, 128), lambda b, g: (b, 0)), lat(gw, zoff),
                  ctx(gw, xoff), ctx(SSD_STATE, boff), ctx(SSD_STATE, coff),
                  pl.BlockSpec((c, 128), lambda b, g: (rb + b, 0)), ctxp(gw, zoff),
                  vec(128), vec(128),
                  pl.BlockSpec((None, 1, gw), lambda b, g: (layer, 0, g))],
        out_specs=[pl.BlockSpec((n, gw), lambda b, g: (b, g)),
                   pl.BlockSpec((c, gw), lambda b, g: (b, g))],
        out_shape=[jax.ShapeDtypeStruct((nb * n, inner), BF16),
                   jax.ShapeDtypeStruct((nb * c, inner), BF16)],
        scratch_shapes=[pltpu.VMEM((c + n, gw), F32),
                        pltpu.VMEM((SSD_STATE, gw), F32), pltpu.VMEM((SSD_STATE, gw), F32),
                        pltpu.VMEM((128, CHUNK), F32), pltpu.VMEM((128, CHUNK), F32)],
        compiler_params=_params(2, 56), name="ssd",
    )(xbc_l, xbc_l, xbc_l, pdt, pa, xbc_c, xbc_c, xbc_c, pdt, pa, dtb, aneg, dskip)


def _gelu_tanh(x):
    return 0.5 * x * (1.0 + jnp.tanh(0.7978845608028654 * (x + 0.044715 * (x * x * x))))


def _lru_kernel(xl, gl, xc, gc, w_ref, b_ref, lam_ref, ul_ref, uc_ref, a_f, h_f, a_b, h_b, *, n, c, rc):
    sp = jax.nn.softplus(-lam_ref[...])
    bias = b_ref[...]

    def gates(x_ref, r0, dst):
        x = x_ref[pl.ds(r0, rc), :]
        xb = x.astype(BF16)
        for sb in range(2):
            lanes = slice(sb * LRU_BW, (sb + 1) * LRU_BW)
            pre = _dot(xb[:, lanes], w_ref[sb])
            for d, (a_s, b_s) in enumerate(((a_f, h_f), (a_b, h_b))):
                r = jax.nn.sigmoid(pre[:, (2 * d) * LRU_BW:(2 * d + 1) * LRU_BW] + bias[2 * d:2 * d + 1, lanes])
                i = jax.nn.sigmoid(pre[:, (2 * d + 1) * LRU_BW:(2 * d + 2) * LRU_BW] + bias[2 * d + 1:2 * d + 2, lanes])
                log_a = -LRU_C * r * sp[d:d + 1, lanes]
                a = jnp.exp(log_a)
                mult = jnp.sqrt(jnp.maximum(1.0 - jnp.exp(2.0 * log_a), 0.0))
                a_s[pl.ds(dst + r0, rc), lanes] = a
                b_s[pl.ds(dst + r0, rc), lanes] = mult * (i * x[:, lanes])

    for ci in range(c // rc):
        gates(xc, ci * rc, 0)

    def gates_lat(i, carry):
        gates(xl, pl.multiple_of(i * rc, rc), c)
        return carry

    lax.fori_loop(0, n // rc, gates_lat, 0)

    def step(t_f, t_b, carry):
        hf, hb = carry
        hf = a_f[pl.ds(t_f, 1), :] * hf + h_f[pl.ds(t_f, 1), :]
        h_f[pl.ds(t_f, 1), :] = hf
        hb = a_b[pl.ds(t_b, 1), :] * hb + h_b[pl.ds(t_b, 1), :]
        h_b[pl.ds(t_b, 1), :] = hb
        return hf, hb

    zero = jnp.zeros((1, xl.shape[1]), F32)
    carry = lax.fori_loop(0, c, lambda i, cr: step(i, c - 1 - i, cr), (zero, zero), unroll=8)
    lax.fori_loop(0, n, lambda i, cr: step(c + i, c + n - 1 - i, cr), carry, unroll=8)

    for ci in range(c // rc):
        r = ci * rc
        uc_ref[r:r + rc, :] = ((h_f[r:r + rc, :] + h_b[r:r + rc, :]) * _gelu_tanh(gc[r:r + rc, :].astype(F32))).astype(BF16)

    def fin_lat(i, carry):
        r = pl.multiple_of(i * rc, rc)
        y = h_f[pl.ds(c + r, rc), :] + h_b[pl.ds(c + r, rc), :]
        ul_ref[pl.ds(r, rc), :] = (y * _gelu_tanh(gl[pl.ds(r, rc), :].astype(F32))).astype(BF16)
        return carry

    lax.fori_loop(0, n // rc, fin_lat, 0)


def rglru(xr_l, xr_c, pb, wcat, gate_b, lam, layer, *, nb, n, c):
    tcw = 2 * LRU_BW
    width = xr_l.shape[1]
    rb = (nb * n) // c
    return pl.pallas_call(
        functools.partial(_lru_kernel, n=n, c=c, rc=128),
        grid=(nb, width // tcw),
        in_specs=[pl.BlockSpec((n, tcw), lambda b, j: (b, j)),
                  pl.BlockSpec((n, tcw), lambda b, j: (b, j)),
                  pl.BlockSpec((c, tcw), lambda b, j: (b, j)),
                  pl.BlockSpec((c, tcw), lambda b, j: (rb + b, j)),
                  pl.BlockSpec((None, 2, LRU_BW, 4 * LRU_BW), lambda b, j: (layer, j, 0, 0)),
                  pl.BlockSpec((None, 4, tcw), lambda b, j: (layer, 0, j)),
                  pl.BlockSpec((None, 2, tcw), lambda b, j: (layer, 0, j))],
        out_specs=[pl.BlockSpec((n, tcw), lambda b, j: (b, j)),
                   pl.BlockSpec((c, tcw), lambda b, j: (b, j))],
        out_shape=[jax.ShapeDtypeStruct((nb * n, width), BF16),
                   jax.ShapeDtypeStruct((nb * c, width), BF16)],
        scratch_shapes=[pltpu.VMEM((c + n, tcw), F32)] * 4,
        compiler_params=_params(2, 56), name="rglru",
    )(xr_l, pb, xr_c, pb, wcat, gate_b, lam)


def _branch_kernel(u0, u1, u2, m0, m1, m2, w_ref, o_ref):
    acc = None
    for k, (u, mr) in enumerate(((u0, m0), (u1, m1), (u2, m2))):
        t = jax.nn.sigmoid(mr[...].astype(F32)) * _dot(u[...], w_ref[k])
        acc = t if acc is None else acc + t
    o_ref[...] = acc.astype(o_ref.dtype)


def branch_merge(u_ret, u_ssd, u_lru, pb, w_branch, layer, *, row_block0, merge_col0, tm, tn):
    m, d = u_ret.shape
    u_spec = pl.BlockSpec((tm, d), lambda j, i: (i, 0))
    m_spec = lambda k: pl.BlockSpec((tm, tn), lambda j, i: (row_block0 + i, (merge_col0 + k * d) // tn + j))
    return pl.pallas_call(
        _branch_kernel,
        grid=(d // tn, m // tm),
        in_specs=[u_spec, u_spec, u_spec, m_spec(0), m_spec(1), m_spec(2),
                  pl.BlockSpec((None, 3, d, tn), lambda j, i: (layer, 0, 0, j))],
        out_specs=pl.BlockSpec((tm, tn), lambda j, i: (i, j)),
        out_shape=jax.ShapeDtypeStruct((m, d), BF16),
        compiler_params=_params(2, 56), name="branch_merge",
    )(u_ret, u_ssd, u_lru, pb, pb, pb, w_branch)


def _ffn_kernel(*refs, gated):
    if gated:
        x_ref, w1_ref, w3_ref, w2_ref, gate_ref, o_ref, acc = refs
    else:
        x_ref, w1_ref, w3_ref, w2_ref, o_ref, acc = refs
    f = pl.program_id(1)

    @pl.when(f == 0)
    def _():
        acc[...] = jnp.zeros_like(acc)

    x = x_ref[...]
    hid = _silu(_dot(x, w1_ref[...].astype(BF16))) * _dot(x, w3_ref[...].astype(BF16))
    if gated:
        hid = hid * gate_ref[...]
    acc[...] += _dot(hid.astype(BF16), w2_ref[...].astype(BF16))

    @pl.when(f == pl.num_programs(1) - 1)
    def _():
        o_ref[...] = acc[...].astype(o_ref.dtype)


def ffn_dense(h, w1, w3, w2, j, *, tm, tf):
    m, d = h.shape
    nf = w1.shape[-1] // tf
    return pl.pallas_call(
        functools.partial(_ffn_kernel, gated=False),
        grid=(m // tm, nf),
        in_specs=[pl.BlockSpec((tm, d), lambda i, f: (i, 0)),
                  pl.BlockSpec((None, d, tf), lambda i, f: (j, 0, f)),
                  pl.BlockSpec((None, d, tf), lambda i, f: (j, 0, f)),
                  pl.BlockSpec((None, tf, d), lambda i, f: (j, f, 0))],
        out_specs=pl.BlockSpec((tm, d), lambda i, f: (i, 0)),
        out_shape=jax.ShapeDtypeStruct((m, d), BF16),
        scratch_shapes=[pltpu.VMEM((tm, d), F32)],
        compiler_params=_params(2, 56), name="ffn_dense",
    )(h, w1, w3, w2)


def ffn_experts(h, gate_t, w1, w3, w2, j, *, tm, tf):
    m, d = h.shape
    n_exp = w1.shape[1]
    nf = w1.shape[-1] // tf
    return pl.pallas_call(
        functools.partial(_ffn_kernel, gated=True),
        grid=(m // tm, n_exp * nf),
        in_specs=[pl.BlockSpec((tm, d), lambda i, s: (i, 0)),
                  pl.BlockSpec((None, None, d, tf), lambda i, s: (j, s // nf, 0, s % nf)),
                  pl.BlockSpec((None, None, d, tf), lambda i, s: (j, s // nf, 0, s % nf)),
                  pl.BlockSpec((None, None, tf, d), lambda i, s: (j, s // nf, s % nf, 0)),
                  pl.BlockSpec((None, tm, 1), lambda i, s: (s // nf, i, 0))],
        out_specs=pl.BlockSpec((tm, d), lambda i, s: (i, 0)),
        out_shape=jax.ShapeDtypeStruct((m, d), BF16),
        scratch_shapes=[pltpu.VMEM((tm, d), F32)],
        compiler_params=_params(2, 56), name="ffn_experts",
    )(h, w1, w3, w2, gate_t)


def _router_kernel(h_ref, w_ref, b_ref, o_ref):
    logits = _dot(h_ref[...], w_ref[...].astype(BF16)) + b_ref[...]
    lane = lax.broadcasted_iota(jnp.int32, logits.shape, 1).astype(F32)
    neg = -jnp.inf
    l1 = jnp.where(lane < N_EXPERTS, logits, neg)
    m1 = jnp.max(l1, axis=-1, keepdims=True)
    i1 = jnp.min(jnp.where(l1 == m1, lane, 1e9), axis=-1, keepdims=True)
    l2 = jnp.where(lane == i1, neg, l1)
    m2 = jnp.max(l2, axis=-1, keepdims=True)
    i2 = jnp.min(jnp.where(l2 == m2, lane, 1e9), axis=-1, keepdims=True)
    e2 = jnp.exp(m2 - m1)
    den = 1.0 + e2
    o_ref[...] = jnp.where(lane == i1, 1.0 / den, 0.0) + jnp.where(lane == i2, e2 / den, 0.0)


def router(h, router_w, router_b, j, *, tm):
    m, d = h.shape
    return pl.pallas_call(
        _router_kernel,
        grid=(m // tm,),
        in_specs=[pl.BlockSpec((tm, d), lambda i: (i, 0)),
                  pl.BlockSpec((None, d, 128), lambda i: (j, 0, 0)),
                  pl.BlockSpec((None, 1, 128), lambda i: (j, 0, 0))],
        out_specs=pl.BlockSpec((tm, 128), lambda i: (i, 0)),
        out_shape=jax.ShapeDtypeStruct((m, 128), F32),
        compiler_params=_params(1, 32), name="router",
    )(h, router_w, router_b)


def kernel(x, c, ctx, c_ctx, w_mod, b_mod, norm1_g, norm2_g, w_in, ret_decay, ret_gn_g, ssd_conv_w, ssd_conv_b, ssd_dt_bias, ssd_a_log, ssd_d, ssd_norm_g, lru_conv_w, lru_conv_b, lru_gate_w, lru_gate_b, lru_lambda, w_branch, w_out, ffn_w1, ffn_w3, ffn_w2, router_w, router_b, moe_w1, moe_w3, moe_w2, final_g):
    nb, n, d = x.shape
    c_len = ctx.shape[1]
    depth = w_mod.shape[0]
    n_lat, n_ctx = nb * n, nb * c_len
    lat_tiles_per_seq = n // 512

    qkvgz_xbc = 11264
    dt_block = qkvgz_xbc // 128
    tail0 = qkvgz_xbc + 2 * SSD_HEADS
    w_tail = w_in[:, :, tail0:].astype(BF16)
    w_br = w_branch.astype(BF16)

    c8 = jnp.concatenate([c, c_ctx[None, :], jnp.zeros((8 - nb - 1, d), F32)], axis=0)
    mods3 = modulation(c8, w_mod, b_mod).reshape(depth * 8 * 6, 1, d)
    seg_lat = lambda i: i // lat_tiles_per_seq
    seg_ctx = lambda i: nb
    log_g = jax.nn.log_sigmoid(ret_decay.astype(F32))
    dtb = jnp.pad(ssd_dt_bias.reshape(depth, 1, 2 * SSD_HEADS), ((0, 0), (0, 0), (0, 128 - 2 * SSD_HEADS)))
    aneg = jnp.pad(-jnp.exp(ssd_a_log.astype(F32)).reshape(depth, 1, 2 * SSD_HEADS),
                   ((0, 0), (0, 0), (0, 128 - 2 * SSD_HEADS)))
    dskip = jnp.repeat(ssd_d, SSD_HEADDIM, axis=-1).reshape(depth, 1, SSD_HEADS * SSD_HEADDIM)
    wcat = jnp.transpose(lru_gate_w, (0, 3, 4, 1, 2, 5)).reshape(depth, 16, LRU_BW, 4 * LRU_BW).astype(BF16)
    gate_b = lru_gate_b.reshape(depth, 4, -1)
    router_wp = jnp.pad(router_w, ((0, 0), (0, 0), (0, 128 - N_EXPERTS)))
    router_bp = jnp.pad(router_b, ((0, 0), (0, 128 - N_EXPERTS)))[:, None, :]

    rows = n // GRID_W
    row = jnp.repeat(jnp.arange(rows, dtype=F32), GRID_W)
    colp = jnp.tile(jnp.arange(GRID_W, dtype=F32), rows)
    n_freq = RET_DK // 4
    inv = ROPE_BASE ** (-jnp.arange(n_freq, dtype=F32) / n_freq)
    ang = jnp.concatenate([row[:, None] * inv, colp[:, None] * inv], axis=-1)
    cos2 = jnp.concatenate([jnp.cos(ang), jnp.cos(ang)], axis=-1)
    sin2 = jnp.concatenate([-jnp.sin(ang), jnp.sin(ang)], axis=-1)

    xl = x.reshape(n_lat, d)
    xc = ctx.reshape(n_ctx, d)
    yl = yc = None
    ctx_rb = n_lat // c_len

    for i in range(depth):
        last = i == depth - 1
        prev_gate = (i - 1, 5)
        g1 = norm1_g[i][None, :]
        xl, hl = resnorm(xl, yl, mods3, g1, seg_fn=seg_lat, gate=prev_gate, sc=(i, 1), sh=(i, 0), out_dtype=BF16, tm=512)
        xc, hc = resnorm(xc, yc, mods3, g1, seg_fn=seg_ctx, gate=prev_gate, sc=(i, 1), sh=(i, 0), out_dtype=BF16, tm=512)
        h = jnp.concatenate([hl, hc], axis=0)
        tm_all = h.shape[0] // 8

        pa = matmul(h, w_in, (i,), col_block0=0, n_cols=qkvgz_xbc, tn=1024, tm=tm_all, out_dtype=BF16)
        pb = matmul(h, w_tail, (i,), col_block0=0, n_cols=w_tail.shape[-1], tn=1024, tm=tm_all, out_dtype=BF16)
        pdt = matmul(h, w_in, (i,), col_block0=dt_block, n_cols=128, tn=128, tm=tm_all, out_dtype=F32)

        xbc_l = dwconv(pa, ssd_conv_w, ssd_conv_b, i, n=n, nb=nb, row_block0=0, col0=8192, n_cols=3072, act=True, out_dtype=BF16)
        xbc_c = dwconv(pa, ssd_conv_w, ssd_conv_b, i, n=c_len, nb=nb, row_block0=ctx_rb, col0=8192, n_cols=3072, act=True, out_dtype=BF16)
        xr_l = dwconv(pb, lru_conv_w, lru_conv_b, i, n=n, nb=nb, row_block0=0, col0=2048, n_cols=2048, act=False, out_dtype=F32)
        xr_c = dwconv(pb, lru_conv_w, lru_conv_b, i, n=c_len, nb=nb, row_block0=ctx_rb, col0=2048, n_cols=2048, act=False, out_dtype=F32)

        ret_l, ret_c = retention(pa, cos2, sin2, log_g[i], ret_gn_g, i, nb=nb, n=n, c=c_len)
        ssd_l, ssd_c = ssd(xbc_l, xbc_c, pdt, pa, dtb, aneg, dskip, i, nb=nb, n=n, c=c_len)
        lru_l, lru_c = rglru(xr_l, xr_c, pb, wcat, gate_b, lru_lambda, i, nb=nb, n=n, c=c_len)
        sg = ssd_norm_g[i][None, :]

        def channel_mix(hh, tm):
            j = i // 2
            if i % 2 == 0:
                return ffn_dense(hh, ffn_w1, ffn_w3, ffn_w2, j, tm=tm, tf=256)
            gate = router(hh, router_wp, router_bp, j, tm=512)
            gate_t = jnp.transpose(gate[:, :N_EXPERTS])[:, :, None]
            return ffn_experts(hh, gate_t, moe_w1, moe_w3, moe_w2, j, tm=tm, tf=256)

        def finish(xres, u_ret, y_ssd, u_lru, seg_fn, row_block0, tm):
            _, u_ssd = resnorm(y_ssd, None, mods3, sg, seg_fn=seg_fn, gate=None, sc=None, sh=None, out_dtype=BF16, tm=512)
            merged = branch_merge(u_ret, u_ssd, u_lru, pb, w_br, i, row_block0=row_block0, merge_col0=4096, tm=tm, tn=512)
            y = matmul(merged, w_out, (i,), col_block0=0, n_cols=d, tn=1024, tm=tm, out_dtype=BF16)
            xres, h2 = resnorm(xres, y, mods3, norm2_g[i][None, :], seg_fn=seg_fn, gate=(i, 2), sc=(i, 4), sh=(i, 3), out_dtype=BF16, tm=512)
            return xres, channel_mix(h2, tm)

        xl, yl = finish(xl, ret_l, ssd_l, lru_l, seg_lat, 0, 1024)
        if not last:
            xc, yc = finish(xc, ret_c, ssd_c, lru_c, seg_ctx, n_lat // 512, 512)

    _, out = resnorm(xl, yl, mods3, final_g[None, :], seg_fn=seg_lat, gate=(depth - 1, 5), sc=None, sh=None, out_dtype=F32, tm=512)
    return out.reshape(nb, n, d)
```

```python
import functools

import jax
import jax.numpy as jnp
from jax import lax
from jax.experimental import pallas as pl
from jax.experimental.pallas import tpu as pltpu

F32 = jnp.float32
BF16 = jnp.bfloat16

EPS = 1e-6
CHUNK = 128
CONV_W = 4
RET_HEADS = 8
RET_DK = 128
RET_DV = 256
SSD_HEADS = 32
SSD_HEADDIM = 64
SSD_GROUPS = 4
SSD_STATE = 128
LRU_BW = 128
LRU_C = 8.0
N_EXPERTS = 8
ROPE_BASE = 10000.0
GRID_W = 64

MIB = 1024 * 1024

_NT = (((1,), (1,)), ((), ()))
_TN = (((0,), (0,)), ((), ()))


def _params(n_axes, vmem_mib):
    return pltpu.CompilerParams(dimension_semantics=("arbitrary",) * n_axes,
                                vmem_limit_bytes=int(vmem_mib * MIB))


def _silu(x):
    return x * jax.nn.sigmoid(x)


def _dot(a, b):
    return jnp.dot(a, b, preferred_element_type=F32)


def _mod_kernel(c_ref, w_ref, b_ref, o_ref):
    x = _silu(c_ref[...]).astype(BF16)
    o_ref[...] = _dot(x, w_ref[...].astype(BF16)) + b_ref[...]


def modulation(c8, w_mod, b_mod):
    depth, d, n6 = w_mod.shape
    tn = 1536
    return pl.pallas_call(
        _mod_kernel,
        grid=(depth, n6 // tn),
        in_specs=[pl.BlockSpec((8, d), lambda l, j: (0, 0)),
                  pl.BlockSpec((None, d, tn), lambda l, j: (l, 0, j)),
                  pl.BlockSpec((None, 1, tn), lambda l, j: (l, 0, j))],
        out_specs=pl.BlockSpec((None, 8, tn), lambda l, j: (l, 0, j)),
        out_shape=jax.ShapeDtypeStruct((depth, 8, n6), F32),
        compiler_params=_params(2, 48),
        name="modulation",
    )(c8, w_mod, b_mod.reshape(depth, 1, n6))


def _resnorm_kernel(*refs, n_y, has_mod):
    refs = list(refs)
    x_ref = refs.pop(0)
    has_y = n_y > 0
    if has_y:
        y_refs = [refs.pop(0) for _ in range(n_y)]
        gate_ref = refs.pop(0)
    g_ref = refs.pop(0)
    if has_mod:
        sc_ref, sh_ref = refs.pop(0), refs.pop(0)
    x = x_ref[...].astype(F32)
    if has_y:
        xo_ref = refs.pop(0)
        y = y_refs[0][...].astype(F32)
        for y_ref in y_refs[1:]:
            y = y + y_ref[...].astype(F32)
        x = x + gate_ref[...] * y
        xo_ref[...] = x
    h_ref = refs.pop(0)
    ms = jnp.mean(x * x, axis=-1, keepdims=True)
    yn = x * lax.rsqrt(ms + EPS) * g_ref[...]
    if has_mod:
        yn = yn * (1.0 + sc_ref[...]) + sh_ref[...]
    h_ref[...] = yn.astype(h_ref.dtype)


def resnorm(x, y, mods3, norm_g, *, seg_fn, gate, sc, sh, out_dtype, tm):
    m, d = x.shape
    if y is None:
        ys = []
    elif isinstance(y, list):
        ys = y
    else:
        ys = [(y, 0)]
    has_y = bool(ys)
    has_mod = sc is not None

    def mod_spec(layer_chunk):
        layer, chunk = layer_chunk
        return pl.BlockSpec((None, 1, d), lambda i: ((layer * 8 + seg_fn(i)) * 6 + chunk, 0, 0))

    row = pl.BlockSpec((tm, d), lambda i: (i, 0))
    in_specs, args = [row], [x]
    if has_y:
        for arr, off in ys:
            in_specs.append(pl.BlockSpec((tm, d), lambda i, off=off: (off + i, 0)))
            args.append(arr)
        in_specs.append(mod_spec(gate))
        args.append(mods3)
    in_specs.append(pl.BlockSpec((1, d), lambda i: (0, 0)))
    args.append(norm_g)
    if has_mod:
        in_specs += [mod_spec(sc), mod_spec(sh)]
        args += [mods3, mods3]
    out_specs, out_shape = [], []
    if has_y:
        out_specs.append(row)
        out_shape.append(jax.ShapeDtypeStruct((m, d), F32))
    out_specs.append(row)
    out_shape.append(jax.ShapeDtypeStruct((m, d), out_dtype))
    res = pl.pallas_call(
        functools.partial(_resnorm_kernel, n_y=len(ys), has_mod=has_mod),
        grid=(m // tm,), in_specs=in_specs, out_specs=out_specs, out_shape=out_shape,
        compiler_params=_params(1, 48), name="resnorm",
    )(*args)
    return (res[0], res[1]) if has_y else (x, res[0])


def _mm_kernel(x_ref, w_ref, o_ref, *scratch):
    if scratch:
        wb = scratch[0]

        @pl.when(pl.program_id(1) == 0)
        def _():
            wb[...] = w_ref[...].astype(BF16)

        w = wb[...]
    else:
        w = w_ref[...]
    o_ref[...] = _dot(x_ref[...], w).astype(o_ref.dtype)


def matmul(x, w, prefix, *, col_block0, n_cols, tn, tm, out_dtype, m_rows=None):
    m, k = x.shape
    m = m_rows or m
    cast = w.dtype != BF16
    lead = (None,) * len(prefix)
    return pl.pallas_call(
        _mm_kernel,
        grid=(n_cols // tn, m // tm),
        in_specs=[pl.BlockSpec((tm, k), lambda j, i: (i, 0)),
                  pl.BlockSpec(lead + (k, tn), lambda j, i: tuple(prefix) + (0, col_block0 + j))],
        out_specs=pl.BlockSpec((tm, tn), lambda j, i: (i, j)),
        out_shape=jax.ShapeDtypeStruct((m, n_cols), out_dtype),
        scratch_shapes=[pltpu.VMEM((k, tn), BF16)] if cast else [],
        compiler_params=_params(2, 56), name="matmul",
    )(x, w)


def _mm_nt_kernel(x_ref, w_ref, o_ref, wb):
    @pl.when(pl.program_id(1) == 0)
    def _():
        wb[...] = w_ref[...].astype(BF16)

    o_ref[...] = lax.dot_general(x_ref[...], wb[...], _NT, preferred_element_type=F32).astype(o_ref.dtype)


def matmul_nt(x, w_t, layer, *, row0, n_cols, tn, tm, out_dtype):
    m, k = x.shape
    return pl.pallas_call(
        _mm_nt_kernel,
        grid=(n_cols // tn, m // tm),
        in_specs=[pl.BlockSpec((tm, k), lambda j, i: (i, 0)),
                  pl.BlockSpec((None, pl.Element(tn), pl.Element(k)),
                               lambda j, i: (layer, pl.multiple_of(row0 + j * tn, 8), 0))],
        out_specs=pl.BlockSpec((tm, tn), lambda j, i: (i, j)),
        out_shape=jax.ShapeDtypeStruct((m, n_cols), out_dtype),
        scratch_shapes=[pltpu.VMEM((tn, k), BF16)],
        compiler_params=_params(2, 56), name="matmul_nt",
    )(x, w_t)


def _conv_kernel(x_ref, w_ref, b_ref, o_ref, scr, *, n, act, rc):
    tc = x_ref.shape[1]
    scr[0:8, :] = jnp.zeros((8, tc), F32)
    scr[8 + n:16 + n, :] = jnp.zeros((8, tc), F32)

    def copy_in(c, carry):
        r0 = pl.multiple_of(c * rc, rc)
        scr[pl.ds(8 + r0, rc), :] = x_ref[pl.ds(r0, rc), :].astype(F32)
        return carry

    lax.fori_loop(0, n // rc, copy_in, 0)
    w = w_ref[...]
    b = b_ref[...]

    def conv(c, carry):
        r0 = pl.multiple_of(c * rc, rc)
        win = scr[pl.ds(r0, rc + 16), :]
        acc = b + w[2:3] * win[8:8 + rc]
        for j in (0, 1, 3):
            rolled = pltpu.roll(win, (2 - j) % (rc + 16), 0)
            acc = acc + w[j:j + 1] * rolled[8:8 + rc]
        if act:
            acc = _silu(acc)
        o_ref[pl.ds(r0, rc), :] = acc.astype(o_ref.dtype)
        return carry

    lax.fori_loop(0, n // rc, conv, 0)


def dwconv(p, conv_w, conv_b, layer, *, n, nb, row_block0, col0, n_cols, act, out_dtype):
    tc = 512
    rc = 64
    depth = conv_w.shape[0]
    return pl.pallas_call(
        functools.partial(_conv_kernel, n=n, act=act, rc=rc),
        grid=(nb, n_cols // tc),
        in_specs=[pl.BlockSpec((n, tc), lambda b, j: (row_block0 + b, col0 // tc + j)),
                  pl.BlockSpec((None, CONV_W, tc), lambda b, j: (layer, 0, j)),
                  pl.BlockSpec((None, 1, tc), lambda b, j: (layer, 0, j))],
        out_specs=pl.BlockSpec((n, tc), lambda b, j: (b, j)),
        out_shape=jax.ShapeDtypeStruct((nb * n, n_cols), out_dtype),
        scratch_shapes=[pltpu.VMEM((n + 16, tc), F32)],
        compiler_params=_params(2, 48), name="dwconv",
    )(p, conv_w, conv_b.reshape(depth, 1, -1))


def _ret_kernel(lg_ref, ql, kl, vl, gl, qc, kc, vc, gc, cos_ref, sin_ref, gn_ref,
                ul_ref, uc_ref, qs, ks, acc, *, n, c):
    h = pl.program_id(1)
    lgf = lg_ref[0, h]
    lgb = lg_ref[1, h]
    ii = lax.broadcasted_iota(jnp.int32, (CHUNK, CHUNK), 0).astype(F32)
    jj = lax.broadcasted_iota(jnp.int32, (CHUNK, CHUNK), 1).astype(F32)
    diff = ii - jj
    dsum = (jnp.where(diff >= 0, jnp.exp(lgf * jnp.maximum(diff, 0.0)), 0.0)
            + jnp.where(diff <= 0, jnp.exp(lgb * jnp.maximum(-diff, 0.0)), 0.0))
    col = lax.broadcasted_iota(jnp.int32, (CHUNK, 1), 0).astype(F32)
    kdec_f = jnp.exp(lgf * (CHUNK - 1.0 - col))
    kdec_b = jnp.exp(lgb * col)
    qdec_f = jnp.exp(lgf * (col + 1.0))
    qdec_b = jnp.exp(lgb * (CHUNK - col))
    one = jnp.ones((1, 1), F32)
    cd_f = jnp.exp(one * (lgf * CHUNK))
    cd_b = jnp.exp(one * (lgb * CHUNK))
    scale = RET_DK ** -0.5

    def fwd_chunk(r0, q, k, v, s_in):
        qs[pl.ds(r0, CHUNK), :] = q
        ks[pl.ds(r0, CHUNK), :] = k
        s = lax.dot_general(q, k, _NT, preferred_element_type=F32)
        y = _dot((s * dsum).astype(BF16), v)
        y = y + _dot((q.astype(F32) * qdec_f).astype(BF16), s_in.astype(BF16))
        kv = lax.dot_general((k.astype(F32) * kdec_f).astype(BF16), v, _TN, preferred_element_type=F32)
        acc[pl.ds(r0, CHUNK), :] = y
        return cd_f * s_in + kv

    def bwd_chunk(r0, v, s_in):
        q = qs[pl.ds(r0, CHUNK), :]
        k = ks[pl.ds(r0, CHUNK), :]
        y = _dot((q.astype(F32) * qdec_b).astype(BF16), s_in.astype(BF16))
        acc[pl.ds(r0, CHUNK), :] = acc[pl.ds(r0, CHUNK), :] + y
        kv = lax.dot_general((k.astype(F32) * kdec_b).astype(BF16), v, _TN, preferred_element_type=F32)
        return cd_b * s_in + kv

    def rot(t_ref, r):
        t = t_ref[pl.ds(r, CHUNK), :].astype(F32)
        return t * cos_ref[pl.ds(r, CHUNK), :] + pltpu.roll(t, RET_DK // 2, 1) * sin_ref[pl.ds(r, CHUNK), :]

    s_f = jnp.zeros((RET_DK, RET_DV), F32)
    for ci in range(c // CHUNK):
        r = ci * CHUNK
        q = qc[r:r + CHUNK, :]
        k = (kc[r:r + CHUNK, :].astype(F32) * scale).astype(BF16)
        s_f = fwd_chunk(r, q, k, vc[r:r + CHUNK, :], s_f)

    def fwd_lat(i, s_in):
        r = pl.multiple_of(i * CHUNK, CHUNK)
        q = rot(ql, r).astype(BF16)
        k = (rot(kl, r) * scale).astype(BF16)
        return fwd_chunk(c + r, q, k, vl[pl.ds(r, CHUNK), :], s_in)

    lax.fori_loop(0, n // CHUNK, fwd_lat, s_f, unroll=2)

    s_b = jnp.zeros((RET_DK, RET_DV), F32)
    for ci in reversed(range(c // CHUNK)):
        r = ci * CHUNK
        s_b = bwd_chunk(r, vc[r:r + CHUNK, :], s_b)

    def bwd_lat(i, s_in):
        r = pl.multiple_of(n - CHUNK - i * CHUNK, CHUNK)
        return bwd_chunk(c + r, vl[pl.ds(r, CHUNK), :], s_in)

    lax.fori_loop(0, n // CHUNK, bwd_lat, s_b, unroll=2)

    gn = gn_ref[...]

    def finish(y, g):
        mu = jnp.mean(y, axis=-1, keepdims=True)
        yc = y - mu
        var = jnp.mean(yc * yc, axis=-1, keepdims=True)
        return (_silu(g.astype(F32)) * (yc * lax.rsqrt(var + EPS) * gn)).astype(BF16)

    for ci in range(c // CHUNK):
        r = ci * CHUNK
        uc_ref[r:r + CHUNK, :] = finish(acc[r:r + CHUNK, :], gc[r:r + CHUNK, :])

    def fin_lat(i, carry):
        r = pl.multiple_of(i * CHUNK, CHUNK)
        ul_ref[pl.ds(r, CHUNK), :] = finish(acc[pl.ds(c + r, CHUNK), :], gl[pl.ds(r, CHUNK), :])
        return carry

    lax.fori_loop(0, n // CHUNK, fin_lat, 0, unroll=2)


def retention(pa, cos2, sin2, log_g, gn_g, layer, *, nb, n, c):
    rb = (nb * n) // c
    lat = lambda w, off: pl.BlockSpec((n, w), lambda b, h: (b, off + h))
    ctx = lambda w, off: pl.BlockSpec((c, w), lambda b, h: (rb + b, off + h))
    full = lambda: pl.BlockSpec((n, RET_DK), lambda b, h: (0, 0))
    depth = gn_g.shape[0]
    return pl.pallas_call(
        functools.partial(_ret_kernel, n=n, c=c),
        grid=(nb, RET_HEADS),
        in_specs=[pl.BlockSpec(memory_space=pltpu.SMEM),
                  lat(RET_DK, 0), lat(RET_DK, 8), lat(RET_DV, 8), lat(RET_DV, 16),
                  ctx(RET_DK, 0), ctx(RET_DK, 8), ctx(RET_DV, 8), ctx(RET_DV, 16),
                  full(), full(),
                  pl.BlockSpec((None, 1, RET_DV), lambda b, h: (layer, 0, h))],
        out_specs=[pl.BlockSpec((n, RET_DV), lambda b, h: (b, h)),
                   pl.BlockSpec((c, RET_DV), lambda b, h: (b, h))],
        out_shape=[jax.ShapeDtypeStruct((nb * n, RET_HEADS * RET_DV), BF16),
                   jax.ShapeDtypeStruct((nb * c, RET_HEADS * RET_DV), BF16)],
        scratch_shapes=[pltpu.VMEM((c + n, RET_DK), BF16), pltpu.VMEM((c + n, RET_DK), BF16),
                        pltpu.VMEM((c + n, RET_DV), F32)],
        compiler_params=_params(2, 48), name="retention",
    )(log_g, pa, pa, pa, pa, pa, pa, pa, pa, cos2, sin2, gn_g.reshape(depth, 1, -1))


def _split3(x):
    hi = x.astype(BF16)
    r = x - hi.astype(F32)
    mid = r.astype(BF16)
    lo = (r - mid.astype(F32)).astype(BF16)
    return hi, mid, lo


def _dot_split_l(x, m, terms):
    parts = _split3(x)[:terms]
    out = _dot(parts[0], m)
    for p in parts[1:]:
        out = out + _dot(p, m)
    return out


def _dot_split_r(m, x, terms):
    parts = _split3(x)[:terms]
    out = _dot(m, parts[0])
    for p in parts[1:]:
        out = out + _dot(m, p)
    return out


def _ssd_kernel(xl, bl, cl, dtl, zl, xc, bc, cc, dtc, zc, dtb_ref, an_ref, dsk_ref,
                yl_ref, yc_ref, acc, s_f, s_b, pt_f, pt_b, *, n, c):
    g = pl.program_id(1)
    L = CHUNK
    hg = SSD_HEADS // SSD_GROUPS
    gw = hg * SSD_HEADDIM
    ii = lax.broadcasted_iota(jnp.int32, (L, L), 0)
    jj = lax.broadcasted_iota(jnp.int32, (L, L), 1)
    causal = ii >= jj
    anti = ii <= jj
    tril = causal.astype(BF16)
    triu = anti.astype(BF16)
    kk = lax.broadcasted_iota(jnp.int32, (L, gw), 0)
    mm = lax.shift_right_logical(lax.broadcasted_iota(jnp.int32, (L, gw), 1), 6)
    e_f = (kk == hg * g + mm).astype(BF16)
    e_b = (kk == SSD_HEADS + hg * g + mm).astype(BF16)
    lane2 = lax.broadcasted_iota(jnp.int32, (L, 2 * SSD_HEADDIM), 1)
    left = lane2 < SSD_HEADDIM
    dtb = dtb_ref[...]
    an = an_ref[...]
    dsk = dsk_ref[...]

    def prep_dir(dt, a, tri, e):
        psm = _dot_split_r(tri, a, 3)
        pexp = _dot_split_l(psm, e, 3)
        dtexp = _dot_split_l(dt, e, 2)
        return psm, pexp, dtexp

    def fwd_chunk(r0, x, bm, cm, dt_raw):
        xf = x.astype(F32)
        dt = jax.nn.softplus(dt_raw + dtb)
        a = dt * an
        psm_f, pexp_f, dtexp_f = prep_dir(dt, a, tril, e_f)
        psm_b, pexp_b, dtexp_b = prep_dir(dt, a, triu, e_b)
        pt_f[...] = psm_f.T
        pt_b[...] = psm_b.T
        xdt_f = xf * dtexp_f
        xdt_b = xf * dtexp_b
        xcat = jnp.concatenate([xdt_f.astype(BF16), xdt_b.astype(BF16)], axis=0)
        cb = lax.dot_general(cm, bm, _NT, preferred_element_type=F32)
        for jp in range(hg // 2):
            lo = jp * 2 * SSD_HEADDIM
            xpair = xcat[:, lo:lo + 2 * SSD_HEADDIM]
            ys = []
            for j in (2 * jp, 2 * jp + 1):
                col = j * SSD_HEADDIM
                seg_f = pexp_f[:, col:col + 1] - pt_f[pl.ds(hg * g + j, 1), :]
                seg_b = pexp_b[:, col:col + 1] - pt_b[pl.ds(SSD_HEADS + hg * g + j, 1), :]
                m_f = cb * jnp.exp(jnp.where(causal, seg_f, -jnp.inf))
                m_b = cb * jnp.exp(jnp.where(anti, seg_b, -jnp.inf))
                mcat = jnp.concatenate([m_f.astype(BF16), m_b.astype(BF16)], axis=1)
                ys.append(_dot(mcat, xpair))
            acc[pl.ds(r0, L), lo:lo + 2 * SSD_HEADDIM] = jnp.where(left, ys[0], ys[1])
        s_in = s_f[...]
        tot = pexp_f[L - 1:L, :]
        y_off = _dot(cm, s_in.astype(BF16)) * jnp.exp(pexp_f)
        st = lax.dot_general(bm, (xdt_f * jnp.exp(tot - pexp_f)).astype(BF16), _TN, preferred_element_type=F32)
        s_f[...] = jnp.exp(tot) * s_in + st
        acc[pl.ds(r0, L), :] = acc[pl.ds(r0, L), :] + y_off + dsk * xf

    def bwd_chunk(r0, x, bm, cm, dt_raw):
        xf = x.astype(F32)
        dt = jax.nn.softplus(dt_raw + dtb)
        a = dt * an
        _, pexp_b, dtexp_b = prep_dir(dt, a, triu, e_b)
        xdt_b = xf * dtexp_b
        s_in = s_b[...]
        tot = pexp_b[0:1, :]
        y_off = _dot(cm, s_in.astype(BF16)) * jnp.exp(pexp_b)
        st = lax.dot_general(bm, (xdt_b * jnp.exp(tot - pexp_b)).astype(BF16), _TN, preferred_element_type=F32)
        s_b[...] = jnp.exp(tot) * s_in + st
        acc[pl.ds(r0, L), :] = acc[pl.ds(r0, L), :] + y_off

    s_f[...] = jnp.zeros_like(s_f)
    s_b[...] = jnp.zeros_like(s_b)
    nc_ctx = c // L
    for ci in range(nc_ctx):
        r = ci * L
        fwd_chunk(r, xc[r:r + L, :], bc[r:r + L, :], cc[r:r + L, :], dtc[r:r + L, :])

    def fwd_lat(i, carry):
        r = pl.multiple_of(i * L, L)
        fwd_chunk(c + r, xl[pl.ds(r, L), :], bl[pl.ds(r, L), :], cl[pl.ds(r, L), :], dtl[pl.ds(r, L), :])
        return carry

    lax.fori_loop(0, n // L, fwd_lat, 0, unroll=2)

    for ci in reversed(range(nc_ctx)):
        r = ci * L
        bwd_chunk(r, xc[r:r + L, :], bc[r:r + L, :], cc[r:r + L, :], dtc[r:r + L, :])

    def bwd_lat(i, carry):
        r = pl.multiple_of(n - L - i * L, L)
        bwd_chunk(c + r, xl[pl.ds(r, L), :], bl[pl.ds(r, L), :], cl[pl.ds(r, L), :], dtl[pl.ds(r, L), :])
        return carry

    lax.fori_loop(0, n // L, bwd_lat, 0, unroll=2)

    for ci in range(nc_ctx):
        r = ci * L
        yc_ref[r:r + L, :] = (acc[r:r + L, :] * _silu(zc[r:r + L, :].astype(F32))).astype(BF16)

    def fin_lat(i, carry):
        r = pl.multiple_of(i * L, L)
        yl_ref[pl.ds(r, L), :] = (acc[pl.ds(c + r, L), :] * _silu(zl[pl.ds(r, L), :].astype(F32))).astype(BF16)
        return carry

    lax.fori_loop(0, n // L, fin_lat, 0)


def ssd(xbc_l, xbc_c, pdt, pa, dtb, aneg, dskip, layer, *, nb, n, c):
    hg = SSD_HEADS // SSD_GROUPS
    gw = hg * SSD_HEADDIM
    rb = (nb * n) // c
    inner = SSD_HEADS * SSD_HEADDIM
    xoff = 0
    boff = inner // SSD_STATE
    coff = boff + SSD_GROUPS
    zoff = 6144 // gw
    lat = lambda w, off: pl.BlockSpec((n, w), lambda b, g: (b, off + g))
    ctx = lambda w, off: pl.BlockSpec((c, w), lambda b, g: (b, off + g))
    ctxp = lambda w, off: pl.BlockSpec((c, w), lambda b, g: (rb + b, off + g))
    vec = lambda w: pl.BlockSpec((None, 1, w), lambda b, g: (layer, 0, 0))
    return pl.pallas_call(
        functools.partial(_ssd_kernel, n=n, c=c),
        grid=(nb, SSD_GROUPS),
        in_specs=[lat(gw, xoff), lat(SSD_STATE, boff), lat(SSD_STATE, coff),
                  pl.BlockSpec((n, 128), lambda b, g: (b, 0)), lat(gw, zoff),
                  ctx(gw, xoff), ctx(SSD_STATE, boff), ctx(SSD_STATE, coff),
                  pl.BlockSpec((c, 128), lambda b, g: (rb + b, 0)), ctxp(gw, zoff),
                  vec(128), vec(128),
                  pl.BlockSpec((None, 1, gw), lambda b, g: (layer, 0, g))],
        out_specs=[pl.BlockSpec((n, gw), lambda b, g: (b, g)),
                   pl.BlockSpec((c, gw), lambda b, g: (b, g))],
        out_shape=[jax.ShapeDtypeStruct((nb * n, inner), BF16),
                   jax.ShapeDtypeStruct((nb * c, inner), BF16)],
        scratch_shapes=[pltpu.VMEM((c + n, gw), F32),
                        pltpu.VMEM((SSD_STATE, gw), F32), pltpu.VMEM((SSD_STATE, gw), F32),
                        pltpu.VMEM((128, CHUNK), F32), pltpu.VMEM((128, CHUNK), F32)],
        compiler_params=_params(2, 56), name="ssd",
    )(xbc_l, xbc_l, xbc_l, pdt, pa, xbc_c, xbc_c, xbc_c, pdt, pa, dtb, aneg, dskip)


def _gelu_tanh(x):
    return 0.5 * x * (1.0 + jnp.tanh(0.7978845608028654 * (x + 0.044715 * (x * x * x))))


def _lru_kernel(xl, gl, xc, gc, w_ref, b_ref, lam_ref, ul_ref, uc_ref, a_f, h_f, a_b, h_b, *, n, c, rc):
    sp = jax.nn.softplus(-lam_ref[...])
    bias = b_ref[...]

    def gates(x_ref, r0, dst):
        x = x_ref[pl.ds(r0, rc), :]
        xb = x.astype(BF16)
        for sb in range(2):
            lanes = slice(sb * LRU_BW, (sb + 1) * LRU_BW)
            pre = _dot(xb[:, lanes], w_ref[sb])
            for d, (a_s, b_s) in enumerate(((a_f, h_f), (a_b, h_b))):
                r = jax.nn.sigmoid(pre[:, (2 * d) * LRU_BW:(2 * d + 1) * LRU_BW] + bias[2 * d:2 * d + 1, lanes])
                i = jax.nn.sigmoid(pre[:, (2 * d + 1) * LRU_BW:(2 * d + 2) * LRU_BW] + bias[2 * d + 1:2 * d + 2, lanes])
                log_a = -LRU_C * r * sp[d:d + 1, lanes]
                a = jnp.exp(log_a)
                mult = jnp.sqrt(jnp.maximum(1.0 - jnp.exp(2.0 * log_a), 0.0))
                a_s[pl.ds(dst + r0, rc), lanes] = a
                b_s[pl.ds(dst + r0, rc), lanes] = mult * (i * x[:, lanes])

    for ci in range(c // rc):
        gates(xc, ci * rc, 0)

    def gates_lat(i, carry):
        gates(xl, pl.multiple_of(i * rc, rc), c)
        return carry

    lax.fori_loop(0, n // rc, gates_lat, 0)

    def step(t_f, t_b, carry):
        hf, hb = carry
        hf = a_f[pl.ds(t_f, 1), :] * hf + h_f[pl.ds(t_f, 1), :]
        h_f[pl.ds(t_f, 1), :] = hf
        hb = a_b[pl.ds(t_b, 1), :] * hb + h_b[pl.ds(t_b, 1), :]
        h_b[pl.ds(t_b, 1), :] = hb
        return hf, hb

    zero = jnp.zeros((1, xl.shape[1]), F32)
    carry = lax.fori_loop(0, c, lambda i, cr: step(i, c - 1 - i, cr), (zero, zero), unroll=8)
    lax.fori_loop(0, n, lambda i, cr: step(c + i, c + n - 1 - i, cr), carry, unroll=8)

    for ci in range(c // rc):
        r = ci * rc
        uc_ref[r:r + rc, :] = ((h_f[r:r + rc, :] + h_b[r:r + rc, :]) * _gelu_tanh(gc[r:r + rc, :].astype(F32))).astype(BF16)

    def fin_lat(i, carry):
        r = pl.multiple_of(i * rc, rc)
        y = h_f[pl.ds(c + r, rc), :] + h_b[pl.ds(c + r, rc), :]
        ul_ref[pl.ds(r, rc), :] = (y * _gelu_tanh(gl[pl.ds(r, rc), :].astype(F32))).astype(BF16)
        return carry

    lax.fori_loop(0, n // rc, fin_lat, 0)


def rglru(xr_l, xr_c, pb, wcat, gate_b, lam, layer, *, nb, n, c):
    tcw = 2 * LRU_BW
    width = xr_l.shape[1]
    rb = (nb * n) // c
    return pl.pallas_call(
        functools.partial(_lru_kernel, n=n, c=c, rc=128),
        grid=(nb, width // tcw),
        in_specs=[pl.BlockSpec((n, tcw), lambda b, j: (b, j)),
                  pl.BlockSpec((n, tcw), lambda b, j: (b, j)),
                  pl.BlockSpec((c, tcw), lambda b, j: (b, j)),
                  pl.BlockSpec((c, tcw), lambda b, j: (rb + b, j)),
                  pl.BlockSpec((None, 2, LRU_BW, 4 * LRU_BW), lambda b, j: (layer, j, 0, 0)),
                  pl.BlockSpec((None, 4, tcw), lambda b, j: (layer, 0, j)),
                  pl.BlockSpec((None, 2, tcw), lambda b, j: (layer, 0, j))],
        out_specs=[pl.BlockSpec((n, tcw), lambda b, j: (b, j)),
                   pl.BlockSpec((c, tcw), lambda b, j: (b, j))],
        out_shape=[jax.ShapeDtypeStruct((nb * n, width), BF16),
                   jax.ShapeDtypeStruct((nb * c, width), BF16)],
        scratch_shapes=[pltpu.VMEM((c + n, tcw), F32)] * 4,
        compiler_params=_params(2, 56), name="rglru",
    )(xr_l, pb, xr_c, pb, wcat, gate_b, lam)


def _branch_kernel(u0, u1, u2, m0, m1, m2, w_ref, o_ref):
    acc = None
    for k, (u, mr) in enumerate(((u0, m0), (u1, m1), (u2, m2))):
        t = jax.nn.sigmoid(mr[...].astype(F32)) * _dot(u[...], w_ref[k])
        acc = t if acc is None else acc + t
    o_ref[...] = acc.astype(o_ref.dtype)


def branch_merge(u_ret, u_ssd, u_lru, pb, w_branch, layer, *, row_block0, merge_col0, tm, tn):
    m, d = u_ret.shape
    u_spec = pl.BlockSpec((tm, d), lambda j, i: (i, 0))
    m_spec = lambda k: pl.BlockSpec((tm, tn), lambda j, i: (row_block0 + i, (merge_col0 + k * d) // tn + j))
    return pl.pallas_call(
        _branch_kernel,
        grid=(d // tn, m // tm),
        in_specs=[u_spec, u_spec, u_spec, m_spec(0), m_spec(1), m_spec(2),
                  pl.BlockSpec((None, 3, d, tn), lambda j, i: (layer, 0, 0, j))],
        out_specs=pl.BlockSpec((tm, tn), lambda j, i: (i, j)),
        out_shape=jax.ShapeDtypeStruct((m, d), BF16),
        compiler_params=_params(2, 56), name="branch_merge",
    )(u_ret, u_ssd, u_lru, pb, pb, pb, w_branch)


def _ffn_kernel(*refs, gated):
    if gated:
        x_ref, w1_ref, w3_ref, w2_ref, gate_ref, o_ref, acc = refs
    else:
        x_ref, w1_ref, w3_ref, w2_ref, o_ref, acc = refs
    f = pl.program_id(1)

    @pl.when(f == 0)
    def _():
        acc[...] = jnp.zeros_like(acc)

    x = x_ref[...]
    hid = _silu(_dot(x, w1_ref[...].astype(BF16))) * _dot(x, w3_ref[...].astype(BF16))
    if gated:
        hid = hid * gate_ref[...]
    acc[...] += _dot(hid.astype(BF16), w2_ref[...].astype(BF16))

    @pl.when(f == pl.num_programs(1) - 1)
    def _():
        o_ref[...] = acc[...].astype(o_ref.dtype)


def ffn_dense(h, w1, w3, w2, j, *, tm, tf):
    m, d = h.shape
    nf = w1.shape[-1] // tf
    return pl.pallas_call(
        functools.partial(_ffn_kernel, gated=False),
        grid=(m // tm, nf),
        in_specs=[pl.BlockSpec((tm, d), lambda i, f: (i, 0)),
                  pl.BlockSpec((None, d, tf), lambda i, f: (j, 0, f)),
                  pl.BlockSpec((None, d, tf), lambda i, f: (j, 0, f)),
                  pl.BlockSpec((None, tf, d), lambda i, f: (j, f, 0))],
        out_specs=pl.BlockSpec((tm, d), lambda i, f: (i, 0)),
        out_shape=jax.ShapeDtypeStruct((m, d), BF16),
        scratch_shapes=[pltpu.VMEM((tm, d), F32)],
        compiler_params=_params(2, 56), name="ffn_dense",
    )(h, w1, w3, w2)


def _moe_kernel(te_ref, nv_ref, src_ref, dst_ref, wgt_ref, x_hbm, w1_ref, w3_ref, w2_ref, y_hbm,
                xg, xb, acc, sem_in, sem_out, *, tm, n_pairs):
    i = pl.program_id(0)
    f = pl.program_id(1)
    nf = pl.num_programs(1)
    valid = i < nv_ref[0]

    def row_in(r):
        return pltpu.make_async_copy(x_hbm.at[pl.ds(src_ref[0, r], 1), :], xg.at[pl.ds(r, 1), :], sem_in)

    def row_out(r):
        return pltpu.make_async_copy(xg.at[pl.ds(r, 1), :], y_hbm.at[pl.ds(dst_ref[0, r], 1), :], sem_out)

    def for_rows(fn):
        def body(r, carry):
            fn(r)
            return carry
        lax.fori_loop(0, tm, body, 0, unroll=8)

    @pl.when(jnp.logical_and(valid, f == 0))
    def _():
        for_rows(lambda r: row_in(r).start())
        for_rows(lambda r: row_in(r).wait())
        xb[...] = xg[...].astype(BF16)
        acc[...] = jnp.zeros_like(acc)

    @pl.when(valid)
    def _():
        x = xb[...]
        hid = _silu(_dot(x, w1_ref[...].astype(BF16))) * _dot(x, w3_ref[...].astype(BF16))
        acc[...] += _dot(hid.astype(BF16), w2_ref[...].astype(BF16))

    def when_real(r, fn):
        @pl.when(dst_ref[0, r] < n_pairs)
        def _():
            fn()

    @pl.when(jnp.logical_and(valid, f == nf - 1))
    def _():
        xg[...] = acc[...] * wgt_ref[...]
        for_rows(lambda r: when_real(r, lambda: row_out(r).start()))
        for_rows(lambda r: when_real(r, lambda: row_out(r).wait()))


def ffn_experts(h, route, w1, w3, w2, j, *, tm, tf):
    m, d = h.shape
    n_exp = w1.shape[1]
    nf = w1.shape[-1] // tf
    n_pairs = 2 * m
    n_tiles = n_pairs // tm + n_exp
    n_slots = n_tiles * tm

    e_flat = route[:, :2].astype(jnp.int32).T.reshape(-1)
    w_flat = route[:, 2:4].T.reshape(-1)
    order = jnp.argsort(e_flat, stable=True).astype(jnp.int32)
    e_sorted = e_flat[order]
    counts = jnp.sum(e_flat[None, :] == jnp.arange(n_exp, dtype=jnp.int32)[:, None], axis=1).astype(jnp.int32)
    starts = jnp.cumsum(counts) - counts
    padded = ((counts + tm - 1) // tm) * tm
    pends = jnp.cumsum(padded)
    pstarts = pends - padded
    slot = pstarts[e_sorted] + (jnp.arange(n_pairs, dtype=jnp.int32) - starts[e_sorted])
    src = jnp.zeros((n_slots,), jnp.int32).at[slot].set(order % m)
    dst = jnp.full((n_slots,), n_pairs, jnp.int32).at[slot].set(order)
    wgt = jnp.zeros((n_slots,), F32).at[slot].set(w_flat[order])
    n_valid = (pends[-1] // tm).astype(jnp.int32)
    tile_start = jnp.arange(n_tiles, dtype=jnp.int32) * tm
    tile_e = jnp.sum(tile_start[:, None] >= pends[None, :], axis=1).astype(jnp.int32)
    tile_e = jnp.minimum(tile_e, tile_e[jnp.maximum(n_valid - 1, 0)])

    def w_idx(i, f, te, nv):
        return jnp.where(i < nv[0], f, nf - 1)

    grid_spec = pltpu.PrefetchScalarGridSpec(
        num_scalar_prefetch=2,
        grid=(n_tiles, nf),
        in_specs=[pl.BlockSpec((None, 1, tm), lambda i, f, te, nv: (i, 0, 0), memory_space=pltpu.SMEM),
                  pl.BlockSpec((None, 1, tm), lambda i, f, te, nv: (i, 0, 0), memory_space=pltpu.SMEM),
                  pl.BlockSpec((tm, 1), lambda i, f, te, nv: (i, 0)),
                  pl.BlockSpec(memory_space=pl.ANY),
                  pl.BlockSpec((None, None, d, tf), lambda i, f, te, nv: (j, te[i], 0, w_idx(i, f, te, nv))),
                  pl.BlockSpec((None, None, d, tf), lambda i, f, te, nv: (j, te[i], 0, w_idx(i, f, te, nv))),
                  pl.BlockSpec((None, None, tf, d), lambda i, f, te, nv: (j, te[i], w_idx(i, f, te, nv), 0))],
        out_specs=pl.BlockSpec(memory_space=pl.ANY),
        scratch_shapes=[pltpu.VMEM((tm, d), F32), pltpu.VMEM((tm, d), BF16), pltpu.VMEM((tm, d), F32),
                        pltpu.SemaphoreType.DMA(()), pltpu.SemaphoreType.DMA(())],
    )
    return pl.pallas_call(
        functools.partial(_moe_kernel, tm=tm, n_pairs=n_pairs),
        grid_spec=grid_spec,
        out_shape=jax.ShapeDtypeStruct((n_pairs, d), F32),
        compiler_params=_params(2, 56), name="ffn_experts",
    )(tile_e, n_valid.reshape(1), src.reshape(n_tiles, 1, tm), dst.reshape(n_tiles, 1, tm),
      wgt.reshape(n_slots, 1), h, w1, w3, w2)


def _router_kernel(h_ref, w_ref, b_ref, o_ref):
    logits = _dot(h_ref[...].astype(BF16), w_ref[...].astype(BF16)) + b_ref[...]
    lane = lax.broadcasted_iota(jnp.int32, logits.shape, 1).astype(F32)
    neg = -jnp.inf
    l1 = jnp.where(lane < N_EXPERTS, logits, neg)
    m1 = jnp.max(l1, axis=-1, keepdims=True)
    i1 = jnp.min(jnp.where(l1 == m1, lane, 1e9), axis=-1, keepdims=True)
    l2 = jnp.where(lane == i1, neg, l1)
    m2 = jnp.max(l2, axis=-1, keepdims=True)
    i2 = jnp.min(jnp.where(l2 == m2, lane, 1e9), axis=-1, keepdims=True)
    e2 = jnp.exp(m2 - m1)
    den = 1.0 + e2
    o_ref[...] = (jnp.where(lane == 0.0, i1, 0.0) + jnp.where(lane == 1.0, i2, 0.0)
                  + jnp.where(lane == 2.0, 1.0 / den, 0.0) + jnp.where(lane == 3.0, e2 / den, 0.0))


def router(h, router_w, router_b, j, *, tm):
    m, d = h.shape
    return pl.pallas_call(
        _router_kernel,
        grid=(m // tm,),
        in_specs=[pl.BlockSpec((tm, d), lambda i: (i, 0)),
                  pl.BlockSpec((None, d, 128), lambda i: (j, 0, 0)),
                  pl.BlockSpec((None, 1, 128), lambda i: (j, 0, 0))],
        out_specs=pl.BlockSpec((tm, 128), lambda i: (i, 0)),
        out_shape=jax.ShapeDtypeStruct((m, 128), F32),
        compiler_params=_params(1, 32), name="router",
    )(h, router_w, router_b)


def kernel(x, c, ctx, c_ctx, w_mod, b_mod, norm1_g, norm2_g, w_in, ret_decay, ret_gn_g, ssd_conv_w, ssd_conv_b, ssd_dt_bias, ssd_a_log, ssd_d, ssd_norm_g, lru_conv_w, lru_conv_b, lru_gate_w, lru_gate_b, lru_lambda, w_branch, w_out, ffn_w1, ffn_w3, ffn_w2, router_w, router_b, moe_w1, moe_w3, moe_w2, final_g):
    nb, n, d = x.shape
    c_len = ctx.shape[1]
    depth = w_mod.shape[0]
    n_lat, n_ctx = nb * n, nb * c_len
    lat_tiles_per_seq = n // 512

    qkvgz_xbc = 11264
    tail0 = qkvgz_xbc + 2 * SSD_HEADS
    n_tail = w_in.shape[-1] - tail0
    w_in_t = jnp.swapaxes(w_in, 1, 2)
    w_br = w_branch.astype(BF16)

    c8 = jnp.concatenate([c, c_ctx[None, :], jnp.zeros((8 - nb - 1, d), F32)], axis=0)
    mods3 = modulation(c8, w_mod, b_mod).reshape(depth * 8 * 6, 1, d)
    seg_lat = lambda i: i // lat_tiles_per_seq
    seg_ctx = lambda i: nb
    log_g = jax.nn.log_sigmoid(ret_decay.astype(F32))
    dtb = jnp.pad(ssd_dt_bias.reshape(depth, 1, 2 * SSD_HEADS), ((0, 0), (0, 0), (0, 128 - 2 * SSD_HEADS)))
    aneg = jnp.pad(-jnp.exp(ssd_a_log.astype(F32)).reshape(depth, 1, 2 * SSD_HEADS),
                   ((0, 0), (0, 0), (0, 128 - 2 * SSD_HEADS)))
    dskip = jnp.repeat(ssd_d, SSD_HEADDIM, axis=-1).reshape(depth, 1, SSD_HEADS * SSD_HEADDIM)
    wcat = jnp.transpose(lru_gate_w, (0, 3, 4, 1, 2, 5)).reshape(depth, 16, LRU_BW, 4 * LRU_BW).astype(BF16)
    gate_b = lru_gate_b.reshape(depth, 4, -1)
    router_wp = jnp.pad(router_w, ((0, 0), (0, 0), (0, 128 - N_EXPERTS)))
    router_bp = jnp.pad(router_b, ((0, 0), (0, 128 - N_EXPERTS)))[:, None, :]

    rows = n // GRID_W
    row = jnp.repeat(jnp.arange(rows, dtype=F32), GRID_W)
    colp = jnp.tile(jnp.arange(GRID_W, dtype=F32), rows)
    n_freq = RET_DK // 4
    inv = ROPE_BASE ** (-jnp.arange(n_freq, dtype=F32) / n_freq)
    ang = jnp.concatenate([row[:, None] * inv, colp[:, None] * inv], axis=-1)
    cos2 = jnp.concatenate([jnp.cos(ang), jnp.cos(ang)], axis=-1)
    sin2 = jnp.concatenate([-jnp.sin(ang), jnp.sin(ang)], axis=-1)

    xl = x.reshape(n_lat, d)
    xc = ctx.reshape(n_ctx, d)
    yl = yc = None
    ctx_rb = n_lat // c_len

    for i in range(depth):
        last = i == depth - 1
        prev_gate = (i - 1, 5)
        g1 = norm1_g[i][None, :]
        xl, hl = resnorm(xl, yl, mods3, g1, seg_fn=seg_lat, gate=prev_gate, sc=(i, 1), sh=(i, 0), out_dtype=BF16, tm=512)
        xc, hc = resnorm(xc, yc, mods3, g1, seg_fn=seg_ctx, gate=prev_gate, sc=(i, 1), sh=(i, 0), out_dtype=BF16, tm=512)
        h = jnp.concatenate([hl, hc], axis=0)
        tm_all = h.shape[0] // 8

        pa = matmul_nt(h, w_in_t, i, row0=0, n_cols=qkvgz_xbc, tn=1024, tm=tm_all, out_dtype=BF16)
        pb = matmul_nt(h, w_in_t, i, row0=tail0, n_cols=n_tail, tn=1024, tm=tm_all, out_dtype=BF16)
        pdt = matmul_nt(h, w_in_t, i, row0=qkvgz_xbc, n_cols=128, tn=128, tm=tm_all, out_dtype=F32)

        xbc_l = dwconv(pa, ssd_conv_w, ssd_conv_b, i, n=n, nb=nb, row_block0=0, col0=8192, n_cols=3072, act=True, out_dtype=BF16)
        xbc_c = dwconv(pa, ssd_conv_w, ssd_conv_b, i, n=c_len, nb=nb, row_block0=ctx_rb, col0=8192, n_cols=3072, act=True, out_dtype=BF16)
        xr_l = dwconv(pb, lru_conv_w, lru_conv_b, i, n=n, nb=nb, row_block0=0, col0=2048, n_cols=2048, act=False, out_dtype=F32)
        xr_c = dwconv(pb, lru_conv_w, lru_conv_b, i, n=c_len, nb=nb, row_block0=ctx_rb, col0=2048, n_cols=2048, act=False, out_dtype=F32)

        ret_l, ret_c = retention(pa, cos2, sin2, log_g[i], ret_gn_g, i, nb=nb, n=n, c=c_len)
        ssd_l, ssd_c = ssd(xbc_l, xbc_c, pdt, pa, dtb, aneg, dskip, i, nb=nb, n=n, c=c_len)
        lru_l, lru_c = rglru(xr_l, xr_c, pb, wcat, gate_b, lru_lambda, i, nb=nb, n=n, c=c_len)
        sg = ssd_norm_g[i][None, :]

        dense = i % 2 == 0
        h2_dtype = BF16 if dense else F32

        def mixer_out(xres, u_ret, y_ssd, u_lru, seg_fn, row_block0, tm):
            _, u_ssd = resnorm(y_ssd, None, mods3, sg, seg_fn=seg_fn, gate=None, sc=None, sh=None, out_dtype=BF16, tm=512)
            merged = branch_merge(u_ret, u_ssd, u_lru, pb, w_br, i, row_block0=row_block0, merge_col0=4096, tm=tm, tn=512)
            y = matmul(merged, w_out, (i,), col_block0=0, n_cols=d, tn=1024, tm=tm, out_dtype=BF16)
            return resnorm(xres, y, mods3, norm2_g[i][None, :], seg_fn=seg_fn, gate=(i, 2), sc=(i, 4), sh=(i, 3),
                           out_dtype=h2_dtype, tm=512)

        xl, h2l = mixer_out(xl, ret_l, ssd_l, lru_l, seg_lat, 0, 1024)
        if not last:
            xc, h2c = mixer_out(xc, ret_c, ssd_c, lru_c, seg_ctx, n_lat // 512, 512)
        j = i // 2
        if dense:
            yl = ffn_dense(h2l, ffn_w1, ffn_w3, ffn_w2, j, tm=1024, tf=256)
            if not last:
                yc = ffn_dense(h2c, ffn_w1, ffn_w3, ffn_w2, j, tm=512, tf=256)
        else:
            h2 = h2l if last else jnp.concatenate([h2l, h2c], axis=0)
            route = router(h2, router_wp, router_bp, j, tm=512)
            y2 = ffn_experts(h2, route, moe_w1, moe_w3, moe_w2, j, tm=512, tf=512)
            mb = h2.shape[0] // 512
            yl = [(y2, 0), (y2, mb)]
            yc = [(y2, n_lat // 512), (y2, mb + n_lat // 512)]

    _, out = resnorm(xl, yl, mods3, final_g[None, :], seg_fn=seg_lat, gate=(depth - 1, 5), sc=None, sh=None, out_dtype=F32, tm=512)
    return out.reshape(nb, n, d)
```

```python
import functools

import jax
import jax.numpy as jnp
from jax import lax
from jax.experimental import pallas as pl
from jax.experimental.pallas import tpu as pltpu

F32 = jnp.float32
BF16 = jnp.bfloat16

EPS = 1e-6
CHUNK = 128
CONV_W = 4
RET_HEADS = 8
RET_DK = 128
RET_DV = 256
SSD_HEADS = 32
SSD_HEADDIM = 64
SSD_GROUPS = 4
SSD_STATE = 128
LRU_BW = 128
LRU_C = 8.0
N_EXPERTS = 8
ROPE_BASE = 10000.0
GRID_W = 64

MIB = 1024 * 1024

_NT = (((1,), (1,)), ((), ()))
_TN = (((0,), (0,)), ((), ()))


def _params(n_axes, vmem_mib):
    return pltpu.CompilerParams(dimension_semantics=("arbitrary",) * n_axes,
                                vmem_limit_bytes=int(vmem_mib * MIB))


def _silu(x):
    return x * jax.nn.sigmoid(x)


def _dot(a, b):
    return jnp.dot(a, b, preferred_element_type=F32)


def _mod_kernel(c_ref, w_ref, b_ref, o_ref):
    x = _silu(c_ref[...]).astype(BF16)
    o_ref[...] = _dot(x, w_ref[...].astype(BF16)) + b_ref[...]


def modulation(c8, w_mod, b_mod):
    depth, d, n6 = w_mod.shape
    tn = 1536
    return pl.pallas_call(
        _mod_kernel,
        grid=(depth, n6 // tn),
        in_specs=[pl.BlockSpec((8, d), lambda l, j: (0, 0)),
                  pl.BlockSpec((None, d, tn), lambda l, j: (l, 0, j)),
                  pl.BlockSpec((None, 1, tn), lambda l, j: (l, 0, j))],
        out_specs=pl.BlockSpec((None, 8, tn), lambda l, j: (l, 0, j)),
        out_shape=jax.ShapeDtypeStruct((depth, 8, n6), F32),
        compiler_params=_params(2, 48),
        name="modulation",
    )(c8, w_mod, b_mod.reshape(depth, 1, n6))


def _resnorm_kernel(*refs, n_y, has_mod):
    refs = list(refs)
    x_ref = refs.pop(0)
    has_y = n_y > 0
    if has_y:
        y_refs = [refs.pop(0) for _ in range(n_y)]
        gate_ref = refs.pop(0)
    g_ref = refs.pop(0)
    if has_mod:
        sc_ref, sh_ref = refs.pop(0), refs.pop(0)
    x = x_ref[...].astype(F32)
    if has_y:
        xo_ref = refs.pop(0)
        y = y_refs[0][...].astype(F32)
        for y_ref in y_refs[1:]:
            y = y + y_ref[...].astype(F32)
        x = x + gate_ref[...] * y
        xo_ref[...] = x
    h_ref = refs.pop(0)
    ms = jnp.mean(x * x, axis=-1, keepdims=True)
    yn = x * lax.rsqrt(ms + EPS) * g_ref[...]
    if has_mod:
        yn = yn * (1.0 + sc_ref[...]) + sh_ref[...]
    h_ref[...] = yn.astype(h_ref.dtype)


def resnorm(x, y, mods3, norm_g, *, seg_fn, gate, sc, sh, out_dtype, tm):
    m, d = x.shape
    if y is None:
        ys = []
    elif isinstance(y, list):
        ys = y
    else:
        ys = [(y, 0)]
    has_y = bool(ys)
    has_mod = sc is not None

    def mod_spec(layer_chunk):
        layer, chunk = layer_chunk
        return pl.BlockSpec((None, 1, d), lambda i: ((layer * 8 + seg_fn(i)) * 6 + chunk, 0, 0))

    row = pl.BlockSpec((tm, d), lambda i: (i, 0))
    in_specs, args = [row], [x]
    if has_y:
        for arr, off in ys:
            in_specs.append(pl.BlockSpec((tm, d), lambda i, off=off: (off + i, 0)))
            args.append(arr)
        in_specs.append(mod_spec(gate))
        args.append(mods3)
    in_specs.append(pl.BlockSpec((1, d), lambda i: (0, 0)))
    args.append(norm_g)
    if has_mod:
        in_specs += [mod_spec(sc), mod_spec(sh)]
        args += [mods3, mods3]
    out_specs, out_shape = [], []
    if has_y:
        out_specs.append(row)
        out_shape.append(jax.ShapeDtypeStruct((m, d), F32))
    out_specs.append(row)
    out_shape.append(jax.ShapeDtypeStruct((m, d), out_dtype))
    res = pl.pallas_call(
        functools.partial(_resnorm_kernel, n_y=len(ys), has_mod=has_mod),
        grid=(m // tm,), in_specs=in_specs, out_specs=out_specs, out_shape=out_shape,
        compiler_params=_params(1, 48), name="resnorm",
    )(*args)
    return (res[0], res[1]) if has_y else (x, res[0])


def _mm_kernel(x_ref, w_ref, o_ref, *scratch):
    if scratch:
        wb = scratch[0]

        @pl.when(pl.program_id(1) == 0)
        def _():
            wb[...] = w_ref[...].astype(BF16)

        w = wb[...]
    else:
        w = w_ref[...]
    o_ref[...] = _dot(x_ref[...], w).astype(o_ref.dtype)


def matmul(x, w, prefix, *, col_block0, n_cols, tn, tm, out_dtype, m_rows=None):
    m, k = x.shape
    m = m_rows or m
    cast = w.dtype != BF16
    lead = (None,) * len(prefix)
    return pl.pallas_call(
        _mm_kernel,
        grid=(n_cols // tn, m // tm),
        in_specs=[pl.BlockSpec((tm, k), lambda j, i: (i, 0)),
                  pl.BlockSpec(lead + (k, tn), lambda j, i: tuple(prefix) + (0, col_block0 + j))],
        out_specs=pl.BlockSpec((tm, tn), lambda j, i: (i, j)),
        out_shape=jax.ShapeDtypeStruct((m, n_cols), out_dtype),
        scratch_shapes=[pltpu.VMEM((k, tn), BF16)] if cast else [],
        compiler_params=_params(2, 56), name="matmul",
    )(x, w)


def _mm_nt_kernel(x_ref, w_ref, o_ref, wb):
    @pl.when(pl.program_id(1) == 0)
    def _():
        wb[...] = w_ref[...].astype(BF16)

    o_ref[...] = lax.dot_general(x_ref[...], wb[...], _NT, preferred_element_type=F32).astype(o_ref.dtype)


def matmul_nt(x, w_t, layer, *, row0, n_cols, tn, tm, out_dtype):
    m, k = x.shape
    return pl.pallas_call(
        _mm_nt_kernel,
        grid=(n_cols // tn, m // tm),
        in_specs=[pl.BlockSpec((tm, k), lambda j, i: (i, 0)),
                  pl.BlockSpec((None, pl.Element(tn), pl.Element(k)),
                               lambda j, i: (layer, pl.multiple_of(row0 + j * tn, 8), 0))],
        out_specs=pl.BlockSpec((tm, tn), lambda j, i: (i, j)),
        out_shape=jax.ShapeDtypeStruct((m, n_cols), out_dtype),
        scratch_shapes=[pltpu.VMEM((tn, k), BF16)],
        compiler_params=_params(2, 56), name="matmul_nt",
    )(x, w_t)


def _conv_kernel(x_ref, w_ref, b_ref, o_ref, scr, *, n, act, rc):
    tc = x_ref.shape[1]
    scr[0:8, :] = jnp.zeros((8, tc), F32)
    scr[8 + n:16 + n, :] = jnp.zeros((8, tc), F32)

    def copy_in(c, carry):
        r0 = pl.multiple_of(c * rc, rc)
        scr[pl.ds(8 + r0, rc), :] = x_ref[pl.ds(r0, rc), :].astype(F32)
        return carry

    lax.fori_loop(0, n // rc, copy_in, 0)
    w = w_ref[...]
    b = b_ref[...]

    def conv(c, carry):
        r0 = pl.multiple_of(c * rc, rc)
        win = scr[pl.ds(r0, rc + 16), :]
        acc = b + w[2:3] * win[8:8 + rc]
        for j in (0, 1, 3):
            rolled = pltpu.roll(win, (2 - j) % (rc + 16), 0)
            acc = acc + w[j:j + 1] * rolled[8:8 + rc]
        if act:
            acc = _silu(acc)
        o_ref[pl.ds(r0, rc), :] = acc.astype(o_ref.dtype)
        return carry

    lax.fori_loop(0, n // rc, conv, 0)


def dwconv(p, conv_w, conv_b, layer, *, n, nb, row_block0, col0, n_cols, act, out_dtype):
    tc = 512
    rc = 64
    depth = conv_w.shape[0]
    return pl.pallas_call(
        functools.partial(_conv_kernel, n=n, act=act, rc=rc),
        grid=(nb, n_cols // tc),
        in_specs=[pl.BlockSpec((n, tc), lambda b, j: (row_block0 + b, col0 // tc + j)),
                  pl.BlockSpec((None, CONV_W, tc), lambda b, j: (layer, 0, j)),
                  pl.BlockSpec((None, 1, tc), lambda b, j: (layer, 0, j))],
        out_specs=pl.BlockSpec((n, tc), lambda b, j: (b, j)),
        out_shape=jax.ShapeDtypeStruct((nb * n, n_cols), out_dtype),
        scratch_shapes=[pltpu.VMEM((n + 16, tc), F32)],
        compiler_params=_params(2, 48), name="dwconv",
    )(p, conv_w, conv_b.reshape(depth, 1, -1))


def _ret_kernel(lg_ref, ql, kl, vl, gl, qc, kc, vc, gc, cos_ref, sin_ref, gn_ref,
                ul_ref, uc_ref, qs, ks, acc, *, n, c):
    h = pl.program_id(1)
    lgf = lg_ref[0, h]
    lgb = lg_ref[1, h]
    ii = lax.broadcasted_iota(jnp.int32, (CHUNK, CHUNK), 0).astype(F32)
    jj = lax.broadcasted_iota(jnp.int32, (CHUNK, CHUNK), 1).astype(F32)
    diff = ii - jj
    dsum = (jnp.where(diff >= 0, jnp.exp(lgf * jnp.maximum(diff, 0.0)), 0.0)
            + jnp.where(diff <= 0, jnp.exp(lgb * jnp.maximum(-diff, 0.0)), 0.0))
    col = lax.broadcasted_iota(jnp.int32, (CHUNK, 1), 0).astype(F32)
    kdec_f = jnp.exp(lgf * (CHUNK - 1.0 - col))
    kdec_b = jnp.exp(lgb * col)
    qdec_f = jnp.exp(lgf * (col + 1.0))
    qdec_b = jnp.exp(lgb * (CHUNK - col))
    one = jnp.ones((1, 1), F32)
    cd_f = jnp.exp(one * (lgf * CHUNK))
    cd_b = jnp.exp(one * (lgb * CHUNK))
    scale = RET_DK ** -0.5

    def fwd_chunk(r0, q, k, v, s_in):
        qs[pl.ds(r0, CHUNK), :] = q
        ks[pl.ds(r0, CHUNK), :] = k
        s = lax.dot_general(q, k, _NT, preferred_element_type=F32)
        y = _dot((s * dsum).astype(BF16), v)
        y = y + _dot((q.astype(F32) * qdec_f).astype(BF16), s_in.astype(BF16))
        kv = lax.dot_general((k.astype(F32) * kdec_f).astype(BF16), v, _TN, preferred_element_type=F32)
        acc[pl.ds(r0, CHUNK), :] = y
        return cd_f * s_in + kv

    def bwd_chunk(r0, v, s_in):
        q = qs[pl.ds(r0, CHUNK), :]
        k = ks[pl.ds(r0, CHUNK), :]
        y = _dot((q.astype(F32) * qdec_b).astype(BF16), s_in.astype(BF16))
        acc[pl.ds(r0, CHUNK), :] = acc[pl.ds(r0, CHUNK), :] + y
        kv = lax.dot_general((k.astype(F32) * kdec_b).astype(BF16), v, _TN, preferred_element_type=F32)
        return cd_b * s_in + kv

    def rot(t_ref, r):
        t = t_ref[pl.ds(r, CHUNK), :].astype(F32)
        return t * cos_ref[pl.ds(r, CHUNK), :] + pltpu.roll(t, RET_DK // 2, 1) * sin_ref[pl.ds(r, CHUNK), :]

    s_f = jnp.zeros((RET_DK, RET_DV), F32)
    for ci in range(c // CHUNK):
        r = ci * CHUNK
        q = qc[r:r + CHUNK, :]
        k = (kc[r:r + CHUNK, :].astype(F32) * scale).astype(BF16)
        s_f = fwd_chunk(r, q, k, vc[r:r + CHUNK, :], s_f)

    def fwd_lat(i, s_in):
        r = pl.multiple_of(i * CHUNK, CHUNK)
        q = rot(ql, r).astype(BF16)
        k = (rot(kl, r) * scale).astype(BF16)
        return fwd_chunk(c + r, q, k, vl[pl.ds(r, CHUNK), :], s_in)

    lax.fori_loop(0, n // CHUNK, fwd_lat, s_f, unroll=4)

    s_b = jnp.zeros((RET_DK, RET_DV), F32)
    for ci in reversed(range(c // CHUNK)):
        r = ci * CHUNK
        s_b = bwd_chunk(r, vc[r:r + CHUNK, :], s_b)

    def bwd_lat(i, s_in):
        r = pl.multiple_of(n - CHUNK - i * CHUNK, CHUNK)
        return bwd_chunk(c + r, vl[pl.ds(r, CHUNK), :], s_in)

    lax.fori_loop(0, n // CHUNK, bwd_lat, s_b, unroll=4)

    gn = gn_ref[...]

    def finish(y, g):
        mu = jnp.mean(y, axis=-1, keepdims=True)
        yc = y - mu
        var = jnp.mean(yc * yc, axis=-1, keepdims=True)
        return (_silu(g.astype(F32)) * (yc * lax.rsqrt(var + EPS) * gn)).astype(BF16)

    for ci in range(c // CHUNK):
        r = ci * CHUNK
        uc_ref[r:r + CHUNK, :] = finish(acc[r:r + CHUNK, :], gc[r:r + CHUNK, :])

    def fin_lat(i, carry):
        r = pl.multiple_of(i * CHUNK, CHUNK)
        ul_ref[pl.ds(r, CHUNK), :] = finish(acc[pl.ds(c + r, CHUNK), :], gl[pl.ds(r, CHUNK), :])
        return carry

    lax.fori_loop(0, n // CHUNK, fin_lat, 0, unroll=4)


def retention(pa, cos2, sin2, log_g, gn_g, layer, *, nb, n, c):
    rb = (nb * n) // c
    lat = lambda w, off: pl.BlockSpec((n, w), lambda b, h: (b, off + h))
    ctx = lambda w, off: pl.BlockSpec((c, w), lambda b, h: (rb + b, off + h))
    full = lambda: pl.BlockSpec((n, RET_DK), lambda b, h: (0, 0))
    depth = gn_g.shape[0]
    return pl.pallas_call(
        functools.partial(_ret_kernel, n=n, c=c),
        grid=(nb, RET_HEADS),
        in_specs=[pl.BlockSpec(memory_space=pltpu.SMEM),
                  lat(RET_DK, 0), lat(RET_DK, 8), lat(RET_DV, 8), lat(RET_DV, 16),
                  ctx(RET_DK, 0), ctx(RET_DK, 8), ctx(RET_DV, 8), ctx(RET_DV, 16),
                  full(), full(),
                  pl.BlockSpec((None, 1, RET_DV), lambda b, h: (layer, 0, h))],
        out_specs=[pl.BlockSpec((n, RET_DV), lambda b, h: (b, h)),
                   pl.BlockSpec((c, RET_DV), lambda b, h: (b, h))],
        out_shape=[jax.ShapeDtypeStruct((nb * n, RET_HEADS * RET_DV), BF16),
                   jax.ShapeDtypeStruct((nb * c, RET_HEADS * RET_DV), BF16)],
        scratch_shapes=[pltpu.VMEM((c + n, RET_DK), BF16), pltpu.VMEM((c + n, RET_DK), BF16),
                        pltpu.VMEM((c + n, RET_DV), F32)],
        compiler_params=_params(2, 48), name="retention",
    )(log_g, pa, pa, pa, pa, pa, pa, pa, pa, cos2, sin2, gn_g.reshape(depth, 1, -1))


def _split3(x):
    hi = x.astype(BF16)
    r = x - hi.astype(F32)
    mid = r.astype(BF16)
    lo = (r - mid.astype(F32)).astype(BF16)
    return hi, mid, lo


def _dot_split_l(x, m, terms):
    parts = _split3(x)[:terms]
    out = _dot(parts[0], m)
    for p in parts[1:]:
        out = out + _dot(p, m)
    return out


def _dot_split_r(m, x, terms):
    parts = _split3(x)[:terms]
    out = _dot(m, parts[0])
    for p in parts[1:]:
        out = out + _dot(m, p)
    return out


def _ssd_kernel(xl, bl, cl, dtl, zl, xc, bc, cc, dtc, zc, dtb_ref, an_ref, dsk_ref,
                yl_ref, yc_ref, acc, s_f, s_b, pt_f, pt_b, xw_b, ex_b, tt_b, *, n, c):
    g = pl.program_id(1)
    L = CHUNK
    hg = SSD_HEADS // SSD_GROUPS
    gw = hg * SSD_HEADDIM
    ii = lax.broadcasted_iota(jnp.int32, (L, L), 0)
    jj = lax.broadcasted_iota(jnp.int32, (L, L), 1)
    causal = ii >= jj
    anti = ii <= jj
    tril = causal.astype(BF16)
    triu = anti.astype(BF16)
    kk = lax.broadcasted_iota(jnp.int32, (L, gw), 0)
    mm = lax.shift_right_logical(lax.broadcasted_iota(jnp.int32, (L, gw), 1), 6)
    e_f = (kk == hg * g + mm).astype(BF16)
    e_b = (kk == SSD_HEADS + hg * g + mm).astype(BF16)
    lane2 = lax.broadcasted_iota(jnp.int32, (L, 2 * SSD_HEADDIM), 1)
    left = lane2 < SSD_HEADDIM
    dtb = dtb_ref[...]
    an = an_ref[...]
    dsk = dsk_ref[...]

    def fwd_chunk(r0, x, bm, cm, dt_raw):
        xf = x.astype(F32)
        dt = jax.nn.softplus(dt_raw + dtb)
        a = dt * an
        psm_f = _dot_split_r(tril, a, 2)
        psm_b = psm_f[L - 1:L, :] - psm_f + a
        pexp_f = _dot_split_l(psm_f, e_f, 2)
        pexp_b = _dot_split_l(psm_b, e_b, 2)
        dtb16 = dt.astype(BF16)
        pt_f[...] = psm_f.T
        pt_b[...] = psm_b.T
        xdt_f = xf * _dot(dtb16, e_f)
        xdt_b = xf * _dot(dtb16, e_b)
        tot_b = pexp_b[0:1, :]
        xw_b[pl.ds(r0, L), :] = (xdt_b * jnp.exp(tot_b - pexp_b)).astype(BF16)
        ex_b[pl.ds(r0, L), :] = jnp.exp(pexp_b).astype(BF16)
        tt_b[pl.ds(r0 // L, 1)] = jnp.broadcast_to(jnp.exp(tot_b), (1, 8, gw))
        xcat = jnp.concatenate([xdt_f.astype(BF16), xdt_b.astype(BF16)], axis=0)
        cb = lax.dot_general(cm, bm, _NT, preferred_element_type=F32)
        for jp in range(hg // 2):
            lo = jp * 2 * SSD_HEADDIM
            xpair = xcat[:, lo:lo + 2 * SSD_HEADDIM]
            ys = []
            for j in (2 * jp, 2 * jp + 1):
                col = j * SSD_HEADDIM
                seg_f = pexp_f[:, col:col + 1] - pt_f[pl.ds(hg * g + j, 1), :]
                seg_b = pexp_b[:, col:col + 1] - pt_b[pl.ds(SSD_HEADS + hg * g + j, 1), :]
                m_f = cb * jnp.exp(jnp.where(causal, seg_f, -jnp.inf))
                m_b = cb * jnp.exp(jnp.where(anti, seg_b, -jnp.inf))
                mcat = jnp.concatenate([m_f.astype(BF16), m_b.astype(BF16)], axis=1)
                ys.append(_dot(mcat, xpair))
            acc[pl.ds(r0, L), lo:lo + 2 * SSD_HEADDIM] = jnp.where(left, ys[0], ys[1])
        s_in = s_f[...]
        tot = pexp_f[L - 1:L, :]
        y_off = _dot(cm, s_in.astype(BF16)) * jnp.exp(pexp_f)
        st = lax.dot_general(bm, (xdt_f * jnp.exp(tot - pexp_f)).astype(BF16), _TN, preferred_element_type=F32)
        s_f[...] = jnp.exp(tot) * s_in + st
        acc[pl.ds(r0, L), :] = acc[pl.ds(r0, L), :] + y_off + dsk * xf

    def bwd_chunk(r0, bm, cm):
        s_in = s_b[...]
        y_off = _dot(cm, s_in.astype(BF16)) * ex_b[pl.ds(r0, L), :].astype(F32)
        st = lax.dot_general(bm, xw_b[pl.ds(r0, L), :], _TN, preferred_element_type=F32)
        s_b[...] = tt_b[r0 // L][0:1, :] * s_in + st
        acc[pl.ds(r0, L), :] = acc[pl.ds(r0, L), :] + y_off

    s_f[...] = jnp.zeros_like(s_f)
    s_b[...] = jnp.zeros_like(s_b)
    nc_ctx = c // L
    for ci in range(nc_ctx):
        r = ci * L
        fwd_chunk(r, xc[r:r + L, :], bc[r:r + L, :], cc[r:r + L, :], dtc[r:r + L, :])

    def fwd_lat(i, carry):
        r = pl.multiple_of(i * L, L)
        fwd_chunk(c + r, xl[pl.ds(r, L), :], bl[pl.ds(r, L), :], cl[pl.ds(r, L), :], dtl[pl.ds(r, L), :])
        return carry

    lax.fori_loop(0, n // L, fwd_lat, 0, unroll=2)

    for ci in reversed(range(nc_ctx)):
        r = ci * L
        bwd_chunk(r, bc[r:r + L, :], cc[r:r + L, :])

    def bwd_lat(i, carry):
        r = pl.multiple_of(n - L - i * L, L)
        bwd_chunk(c + r, bl[pl.ds(r, L), :], cl[pl.ds(r, L), :])
        return carry

    lax.fori_loop(0, n // L, bwd_lat, 0, unroll=4)

    for ci in range(nc_ctx):
        r = ci * L
        yc_ref[r:r + L, :] = (acc[r:r + L, :] * _silu(zc[r:r + L, :].astype(F32))).astype(BF16)

    def fin_lat(i, carry):
        r = pl.multiple_of(i * L, L)
        yl_ref[pl.ds(r, L), :] = (acc[pl.ds(c + r, L), :] * _silu(zl[pl.ds(r, L), :].astype(F32))).astype(BF16)
        return carry

    lax.fori_loop(0, n // L, fin_lat, 0)


def ssd(xbc_l, xbc_c, pdt, pa, dtb, aneg, dskip, layer, *, nb, n, c):
    hg = SSD_HEADS // SSD_GROUPS
    gw = hg * SSD_HEADDIM
    rb = (nb * n) // c
    inner = SSD_HEADS * SSD_HEADDIM
    xoff = 0
    boff = inner // SSD_STATE
    coff = boff + SSD_GROUPS
    zoff = 6144 // gw
    lat = lambda w, off: pl.BlockSpec((n, w), lambda b, g: (b, off + g))
    ctx = lambda w, off: pl.BlockSpec((c, w), lambda b, g: (b, off + g))
    ctxp = lambda w, off: pl.BlockSpec((c, w), lambda b, g: (rb + b, off + g))
    vec = lambda w: pl.BlockSpec((None, 1, w), lambda b, g: (layer, 0, 0))
    return pl.pallas_call(
        functools.partial(_ssd_kernel, n=n, c=c),
        grid=(nb, SSD_GROUPS),
        in_specs=[lat(gw, xoff), lat(SSD_STATE, boff), lat(SSD_STATE, coff),
                  pl.BlockSpec((n, 128), lambda b, g: (b, 0)), lat(gw, zoff),
                  ctx(gw, xoff), ctx(SSD_STATE, boff), ctx(SSD_STATE, coff),
                  pl.BlockSpec((c, 128), lambda b, g: (rb + b, 0)), ctxp(gw, zoff),
                  vec(128), vec(128),
                  pl.BlockSpec((None, 1, gw), lambda b, g: (layer, 0, g))],
        out_specs=[pl.BlockSpec((n, gw), lambda b, g: (b, g)),
                   pl.BlockSpec((c, gw), lambda b, g: (b, g))],
        out_shape=[jax.ShapeDtypeStruct((nb * n, inner), BF16),
                   jax.ShapeDtypeStruct((nb * c, inner), BF16)],
        scratch_shapes=[pltpu.VMEM((c + n, gw), F32),
                        pltpu.VMEM((SSD_STATE, gw), F32), pltpu.VMEM((SSD_STATE, gw), F32),
                        pltpu.VMEM((128, CHUNK), F32), pltpu.VMEM((128, CHUNK), F32),
                        pltpu.VMEM((c + n, gw), BF16), pltpu.VMEM((c + n, gw), BF16),
                        pltpu.VMEM(((c + n) // CHUNK, 8, gw), F32)],
        compiler_params=_params(2, 60), name="ssd",
    )(xbc_l, xbc_l, xbc_l, pdt, pa, xbc_c, xbc_c, xbc_c, pdt, pa, dtb, aneg, dskip)


def _gelu_tanh(x):
    return 0.5 * x * (1.0 + jnp.tanh(0.7978845608028654 * (x + 0.044715 * (x * x * x))))


LRU_SEGS = 8


def _lru_kernel(xl, gl, xc, gc, w_ref, b_ref, lam_ref, ul_ref, uc_ref,
                ca_f, ch_f, ca_b, ch_b, p_f, h_f, p_b, h_b, cin, *, n, c, rc):
    seg = n // LRU_SEGS
    nsb = xl.shape[1] // LRU_BW
    sp = jax.nn.softplus(-lam_ref[...])
    bias = b_ref[...]

    def gates(x_ref, r0, store):
        x = x_ref[pl.ds(r0, rc), :]
        xb = x.astype(BF16)
        for sb in range(nsb):
            lanes = slice(sb * LRU_BW, (sb + 1) * LRU_BW)
            pre = _dot(xb[:, lanes], w_ref[sb])
            for d in range(2):
                r = jax.nn.sigmoid(pre[:, (2 * d) * LRU_BW:(2 * d + 1) * LRU_BW] + bias[2 * d:2 * d + 1, lanes])
                i = jax.nn.sigmoid(pre[:, (2 * d + 1) * LRU_BW:(2 * d + 2) * LRU_BW] + bias[2 * d + 1:2 * d + 2, lanes])
                log_a = -LRU_C * r * sp[d:d + 1, lanes]
                a = jnp.exp(log_a)
                mult = jnp.sqrt(jnp.maximum(1.0 - a * a, 0.0))
                store(d, sb, a, mult * (i * x[:, lanes]))

    def store_ctx(r0):
        def store(d, sb, a, bx):
            lanes = slice(sb * LRU_BW, (sb + 1) * LRU_BW)
            (ca_f, ca_b)[d][pl.ds(r0, rc), lanes] = a
            (ch_f, ch_b)[d][pl.ds(r0, rc), lanes] = bx
        return store

    for ci in range(c // rc):
        gates(xc, ci * rc, store_ctx(ci * rc))

    def ctx_step(i, carry):
        hf, hb = carry
        t_b = c - 1 - i
        hf = ca_f[pl.ds(i, 1), :] * hf + ch_f[pl.ds(i, 1), :]
        ch_f[pl.ds(i, 1), :] = hf
        hb = ca_b[pl.ds(t_b, 1), :] * hb + ch_b[pl.ds(t_b, 1), :]
        ch_b[pl.ds(t_b, 1), :] = hb
        return hf, hb

    zero = jnp.zeros((1, xl.shape[1]), F32)
    hf_ctx, hb_ctx = lax.fori_loop(0, c, ctx_step, (zero, zero), unroll=8)

    for ci in range(c // rc):
        r = ci * rc
        uc_ref[r:r + rc, :] = ((ch_f[r:r + rc, :] + ch_b[r:r + rc, :]) * _gelu_tanh(gc[r:r + rc, :].astype(F32))).astype(BF16)

    def gates_lat(ci, carry):
        r0 = pl.multiple_of(ci * rc, rc)
        s = r0 // seg
        row0 = (r0 - s * seg) * LRU_SEGS + s

        def store(d, sb, a, bx):
            (p_f, p_b)[d][sb, pl.ds(row0, rc, stride=LRU_SEGS), :] = a
            (h_f, h_b)[d][sb, pl.ds(row0, rc, stride=LRU_SEGS), :] = bx

        gates(xl, r0, store)
        return carry

    lax.fori_loop(0, n // rc, gates_lat, 0, unroll=2)

    def lat_step(i, carry):
        out = []
        for d, (p_s, h_s) in enumerate(((p_f, h_f), (p_b, h_b))):
            tt = i if d == 0 else seg - 1 - i
            row = pl.multiple_of(tt * LRU_SEGS, LRU_SEGS)
            for sb in range(nsb):
                h, p = carry[d * nsb + sb]
                a = p_s[sb, pl.ds(row, LRU_SEGS), :]
                h = a * h + h_s[sb, pl.ds(row, LRU_SEGS), :]
                p = a * p
                h_s[sb, pl.ds(row, LRU_SEGS), :] = h
                p_s[sb, pl.ds(row, LRU_SEGS), :] = p
                out.append((h, p))
        return tuple(out)

    tile0 = (jnp.zeros((LRU_SEGS, LRU_BW), F32), jnp.ones((LRU_SEGS, LRU_BW), F32))
    ends = lax.fori_loop(0, seg, lat_step, (tile0,) * (2 * nsb), unroll=4)

    sub = lax.broadcasted_iota(jnp.int32, (LRU_SEGS, LRU_BW), 0)
    for d in range(2):
        for sb in range(nsb):
            lanes = slice(sb * LRU_BW, (sb + 1) * LRU_BW)
            h_end, p_end = ends[d * nsb + sb]
            cur = jnp.broadcast_to((hf_ctx, hb_ctx)[d][:, lanes], (LRU_SEGS, LRU_BW))
            cmat = jnp.zeros((LRU_SEGS, LRU_BW), F32)
            order = range(LRU_SEGS) if d == 0 else reversed(range(LRU_SEGS))
            for s in order:
                cmat = jnp.where(sub == s, cur, cmat)
                nxt = h_end + p_end * cur
                cur = jnp.broadcast_to(nxt[s:s + 1, :], (LRU_SEGS, LRU_BW))
            cin[d * nsb + sb] = cmat

    def fin_lat(ci, carry):
        r0 = pl.multiple_of(ci * rc, rc)
        s = r0 // seg
        row0 = (r0 - s * seg) * LRU_SEGS + s
        g = _gelu_tanh(gl[pl.ds(r0, rc), :].astype(F32))
        for sb in range(nsb):
            lanes = slice(sb * LRU_BW, (sb + 1) * LRU_BW)
            y = None
            for d, (p_s, h_s) in enumerate(((p_f, h_f), (p_b, h_b))):
                rows = pl.ds(row0, rc, stride=LRU_SEGS)
                t = h_s[sb, rows, :] + p_s[sb, rows, :] * cin[d * nsb + sb, pl.ds(s, 1), :]
                y = t if y is None else y + t
            ul_ref[pl.ds(r0, rc), lanes] = (y * g[:, lanes]).astype(BF16)
        return carry

    lax.fori_loop(0, n // rc, fin_lat, 0)


def rglru(xr_l, xr_c, pb, wcat, gate_b, lam, layer, *, nb, n, c):
    tcw = 2 * LRU_BW
    width = xr_l.shape[1]
    rb = (nb * n) // c
    return pl.pallas_call(
        functools.partial(_lru_kernel, n=n, c=c, rc=min(128, n // LRU_SEGS)),
        grid=(nb, width // tcw),
        in_specs=[pl.BlockSpec((n, tcw), lambda b, j: (b, j)),
                  pl.BlockSpec((n, tcw), lambda b, j: (b, j)),
                  pl.BlockSpec((c, tcw), lambda b, j: (b, j)),
                  pl.BlockSpec((c, tcw), lambda b, j: (rb + b, j)),
                  pl.BlockSpec((None, 2, LRU_BW, 4 * LRU_BW), lambda b, j: (layer, j, 0, 0)),
                  pl.BlockSpec((None, 4, tcw), lambda b, j: (layer, 0, j)),
                  pl.BlockSpec((None, 2, tcw), lambda b, j: (layer, 0, j))],
        out_specs=[pl.BlockSpec((n, tcw), lambda b, j: (b, j)),
                   pl.BlockSpec((c, tcw), lambda b, j: (b, j))],
        out_shape=[jax.ShapeDtypeStruct((nb * n, width), BF16),
                   jax.ShapeDtypeStruct((nb * c, width), BF16)],
        scratch_shapes=([pltpu.VMEM((c, tcw), F32)] * 4
                        + [pltpu.VMEM((tcw // LRU_BW, n, LRU_BW), F32)] * 4
                        + [pltpu.VMEM((2 * tcw // LRU_BW, LRU_SEGS, LRU_BW), F32)]),
        compiler_params=_params(2, 56), name="rglru",
    )(xr_l, pb, xr_c, pb, wcat, gate_b, lam)


def _branch_kernel(u0, u1, u2, m0, m1, m2, w_ref, o_ref):
    acc = None
    for k, (u, mr) in enumerate(((u0, m0), (u1, m1), (u2, m2))):
        t = jax.nn.sigmoid(mr[...].astype(F32)) * _dot(u[...], w_ref[k])
        acc = t if acc is None else acc + t
    o_ref[...] = acc.astype(o_ref.dtype)


def branch_merge(u_ret, u_ssd, u_lru, pb, w_branch, layer, *, row_block0, merge_col0, tm, tn):
    m, d = u_ret.shape
    u_spec = pl.BlockSpec((tm, d), lambda j, i: (i, 0))
    m_spec = lambda k: pl.BlockSpec((tm, tn), lambda j, i: (row_block0 + i, (merge_col0 + k * d) // tn + j))
    return pl.pallas_call(
        _branch_kernel,
        grid=(d // tn, m // tm),
        in_specs=[u_spec, u_spec, u_spec, m_spec(0), m_spec(1), m_spec(2),
                  pl.BlockSpec((None, 3, d, tn), lambda j, i: (layer, 0, 0, j))],
        out_specs=pl.BlockSpec((tm, tn), lambda j, i: (i, j)),
        out_shape=jax.ShapeDtypeStruct((m, d), BF16),
        compiler_params=_params(2, 56), name="branch_merge",
    )(u_ret, u_ssd, u_lru, pb, pb, pb, w_branch)


def _ffn_kernel(*refs, gated):
    if gated:
        x_ref, w1_ref, w3_ref, w2_ref, gate_ref, o_ref, acc = refs
    else:
        x_ref, w1_ref, w3_ref, w2_ref, o_ref, acc = refs
    f = pl.program_id(1)

    @pl.when(f == 0)
    def _():
        acc[...] = jnp.zeros_like(acc)

    x = x_ref[...]
    hid = _silu(_dot(x, w1_ref[...].astype(BF16))) * _dot(x, w3_ref[...].astype(BF16))
    if gated:
        hid = hid * gate_ref[...]
    acc[...] += _dot(hid.astype(BF16), w2_ref[...].astype(BF16))

    @pl.when(f == pl.num_programs(1) - 1)
    def _():
        o_ref[...] = acc[...].astype(o_ref.dtype)


def ffn_dense(h, w1, w3, w2, j, *, tm, tf):
    m, d = h.shape
    nf = w1.shape[-1] // tf
    return pl.pallas_call(
        functools.partial(_ffn_kernel, gated=False),
        grid=(m // tm, nf),
        in_specs=[pl.BlockSpec((tm, d), lambda i, f: (i, 0)),
                  pl.BlockSpec((None, d, tf), lambda i, f: (j, 0, f)),
                  pl.BlockSpec((None, d, tf), lambda i, f: (j, 0, f)),
                  pl.BlockSpec((None, tf, d), lambda i, f: (j, f, 0))],
        out_specs=pl.BlockSpec((tm, d), lambda i, f: (i, 0)),
        out_shape=jax.ShapeDtypeStruct((m, d), BF16),
        scratch_shapes=[pltpu.VMEM((tm, d), F32)],
        compiler_params=_params(2, 56), name="ffn_dense",
    )(h, w1, w3, w2)


def _moe_kernel(te_ref, nv_ref, src_ref, dst_ref, wgt_ref, x_hbm, w1_ref, w3_ref, w2_ref, y_hbm,
                xg, xb, acc, sem_in, sem_out, *, tm, n_pairs):
    i = pl.program_id(0)
    f = pl.program_id(1)
    nf = pl.num_programs(1)
    valid = i < nv_ref[0]

    def row_in(r):
        return pltpu.make_async_copy(x_hbm.at[pl.ds(src_ref[0, r], 1), :], xg.at[pl.ds(r, 1), :], sem_in)

    def row_out(r):
        return pltpu.make_async_copy(xg.at[pl.ds(r, 1), :], y_hbm.at[pl.ds(dst_ref[0, r], 1), :], sem_out)

    def for_rows(fn):
        def body(r, carry):
            fn(r)
            return carry
        lax.fori_loop(0, tm, body, 0, unroll=8)

    @pl.when(jnp.logical_and(valid, f == 0))
    def _():
        for_rows(lambda r: row_in(r).start())
        for_rows(lambda r: row_in(r).wait())
        xb[...] = xg[...].astype(BF16)
        acc[...] = jnp.zeros_like(acc)

    @pl.when(valid)
    def _():
        x = xb[...]
        hid = _silu(_dot(x, w1_ref[...].astype(BF16))) * _dot(x, w3_ref[...].astype(BF16))
        acc[...] += _dot(hid.astype(BF16), w2_ref[...].astype(BF16))

    def when_real(r, fn):
        @pl.when(dst_ref[0, r] < n_pairs)
        def _():
            fn()

    @pl.when(jnp.logical_and(valid, f == nf - 1))
    def _():
        xg[...] = acc[...] * wgt_ref[...]
        for_rows(lambda r: when_real(r, lambda: row_out(r).start()))
        for_rows(lambda r: when_real(r, lambda: row_out(r).wait()))


def ffn_experts(h, route, w1, w3, w2, j, *, tm, tf):
    m, d = h.shape
    n_exp = w1.shape[1]
    nf = w1.shape[-1] // tf
    n_pairs = 2 * m
    n_tiles = n_pairs // tm + n_exp
    n_slots = n_tiles * tm

    e_flat = route[:, :2].astype(jnp.int32).T.reshape(-1)
    w_flat = route[:, 2:4].T.reshape(-1)
    order = jnp.argsort(e_flat, stable=True).astype(jnp.int32)
    counts = jnp.sum(e_flat[None, :] == jnp.arange(n_exp, dtype=jnp.int32)[:, None], axis=1).astype(jnp.int32)
    starts = jnp.cumsum(counts) - counts
    padded = ((counts + tm - 1) // tm) * tm
    pends = jnp.cumsum(padded)
    pstarts = pends - padded
    n_valid = (pends[-1] // tm).astype(jnp.int32)
    tile_start = jnp.arange(n_tiles, dtype=jnp.int32) * tm
    tile_e_raw = jnp.sum(tile_start[:, None] >= pends[None, :], axis=1).astype(jnp.int32)
    tile_e = jnp.minimum(tile_e_raw, tile_e_raw[jnp.maximum(n_valid - 1, 0)])
    slot_e = jnp.repeat(tile_e_raw, tm)
    slot_ec = jnp.minimum(slot_e, n_exp - 1)
    rank = jnp.arange(n_slots, dtype=jnp.int32) - pstarts[slot_ec]
    real = jnp.logical_and(slot_e < n_exp, rank < counts[slot_ec])
    pair = order[jnp.clip(starts[slot_ec] + rank, 0, n_pairs - 1)]
    src = jnp.where(real, pair % m, 0)
    dst = jnp.where(real, pair, n_pairs)
    wgt = jnp.where(real, w_flat[pair], 0.0)

    def w_idx(i, f, te, nv):
        return jnp.where(i < nv[0], f, nf - 1)

    grid_spec = pltpu.PrefetchScalarGridSpec(
        num_scalar_prefetch=2,
        grid=(n_tiles, nf),
        in_specs=[pl.BlockSpec((None, 1, tm), lambda i, f, te, nv: (i, 0, 0), memory_space=pltpu.SMEM),
                  pl.BlockSpec((None, 1, tm), lambda i, f, te, nv: (i, 0, 0), memory_space=pltpu.SMEM),
                  pl.BlockSpec((tm, 1), lambda i, f, te, nv: (i, 0)),
                  pl.BlockSpec(memory_space=pl.ANY),
                  pl.BlockSpec((None, None, d, tf), lambda i, f, te, nv: (j, te[i], 0, w_idx(i, f, te, nv))),
                  pl.BlockSpec((None, None, d, tf), lambda i, f, te, nv: (j, te[i], 0, w_idx(i, f, te, nv))),
                  pl.BlockSpec((None, None, tf, d), lambda i, f, te, nv: (j, te[i], w_idx(i, f, te, nv), 0))],
        out_specs=pl.BlockSpec(memory_space=pl.ANY),
        scratch_shapes=[pltpu.VMEM((tm, d), F32), pltpu.VMEM((tm, d), BF16), pltpu.VMEM((tm, d), F32),
                        pltpu.SemaphoreType.DMA(()), pltpu.SemaphoreType.DMA(())],
    )
    return pl.pallas_call(
        functools.partial(_moe_kernel, tm=tm, n_pairs=n_pairs),
        grid_spec=grid_spec,
        out_shape=jax.ShapeDtypeStruct((n_pairs, d), F32),
        compiler_params=_params(2, 56), name="ffn_experts",
    )(tile_e, n_valid.reshape(1), src.reshape(n_tiles, 1, tm), dst.reshape(n_tiles, 1, tm),
      wgt.reshape(n_slots, 1), h, w1, w3, w2)


def _router_kernel(h_ref, w_ref, b_ref, o_ref):
    logits = _dot(h_ref[...].astype(BF16), w_ref[...].astype(BF16)) + b_ref[...]
    lane = lax.broadcasted_iota(jnp.int32, logits.shape, 1).astype(F32)
    neg = -jnp.inf
    l1 = jnp.where(lane < N_EXPERTS, logits, neg)
    m1 = jnp.max(l1, axis=-1, keepdims=True)
    i1 = jnp.min(jnp.where(l1 == m1, lane, 1e9), axis=-1, keepdims=True)
    l2 = jnp.where(lane == i1, neg, l1)
    m2 = jnp.max(l2, axis=-1, keepdims=True)
    i2 = jnp.min(jnp.where(l2 == m2, lane, 1e9), axis=-1, keepdims=True)
    e2 = jnp.exp(m2 - m1)
    den = 1.0 + e2
    o_ref[...] = (jnp.where(lane == 0.0, i1, 0.0) + jnp.where(lane == 1.0, i2, 0.0)
                  + jnp.where(lane == 2.0, 1.0 / den, 0.0) + jnp.where(lane == 3.0, e2 / den, 0.0))


def router(h, router_w, router_b, j, *, tm):
    m, d = h.shape
    return pl.pallas_call(
        _router_kernel,
        grid=(m // tm,),
        in_specs=[pl.BlockSpec((tm, d), lambda i: (i, 0)),
                  pl.BlockSpec((None, d, 128), lambda i: (j, 0, 0)),
                  pl.BlockSpec((None, 1, 128), lambda i: (j, 0, 0))],
        out_specs=pl.BlockSpec((tm, 128), lambda i: (i, 0)),
        out_shape=jax.ShapeDtypeStruct((m, 128), F32),
        compiler_params=_params(1, 32), name="router",
    )(h, router_w, router_b)


def kernel(x, c, ctx, c_ctx, w_mod, b_mod, norm1_g, norm2_g, w_in, ret_decay, ret_gn_g, ssd_conv_w, ssd_conv_b, ssd_dt_bias, ssd_a_log, ssd_d, ssd_norm_g, lru_conv_w, lru_conv_b, lru_gate_w, lru_gate_b, lru_lambda, w_branch, w_out, ffn_w1, ffn_w3, ffn_w2, router_w, router_b, moe_w1, moe_w3, moe_w2, final_g):
    nb, n, d = x.shape
    c_len = ctx.shape[1]
    depth = w_mod.shape[0]
    n_lat, n_ctx = nb * n, nb * c_len
    lat_tiles_per_seq = n // 512

    qkvgz_xbc = 11264
    tail0 = qkvgz_xbc + 2 * SSD_HEADS
    n_tail = w_in.shape[-1] - tail0
    w_in_t = jnp.swapaxes(w_in, 1, 2)
    w_br = w_branch.astype(BF16)

    c8 = jnp.concatenate([c, c_ctx[None, :], jnp.zeros((8 - nb - 1, d), F32)], axis=0)
    mods3 = modulation(c8, w_mod, b_mod).reshape(depth * 8 * 6, 1, d)
    seg_lat = lambda i: i // lat_tiles_per_seq
    seg_ctx = lambda i: nb
    log_g = jax.nn.log_sigmoid(ret_decay.astype(F32))
    dtb = jnp.pad(ssd_dt_bias.reshape(depth, 1, 2 * SSD_HEADS), ((0, 0), (0, 0), (0, 128 - 2 * SSD_HEADS)))
    aneg = jnp.pad(-jnp.exp(ssd_a_log.astype(F32)).reshape(depth, 1, 2 * SSD_HEADS),
                   ((0, 0), (0, 0), (0, 128 - 2 * SSD_HEADS)))
    dskip = jnp.repeat(ssd_d, SSD_HEADDIM, axis=-1).reshape(depth, 1, SSD_HEADS * SSD_HEADDIM)
    wcat = jnp.transpose(lru_gate_w, (0, 3, 4, 1, 2, 5)).reshape(depth, 16, LRU_BW, 4 * LRU_BW).astype(BF16)
    gate_b = lru_gate_b.reshape(depth, 4, -1)
    router_wp = jnp.pad(router_w, ((0, 0), (0, 0), (0, 128 - N_EXPERTS)))
    router_bp = jnp.pad(router_b, ((0, 0), (0, 128 - N_EXPERTS)))[:, None, :]

    rows = n // GRID_W
    row = jnp.repeat(jnp.arange(rows, dtype=F32), GRID_W)
    colp = jnp.tile(jnp.arange(GRID_W, dtype=F32), rows)
    n_freq = RET_DK // 4
    inv = ROPE_BASE ** (-jnp.arange(n_freq, dtype=F32) / n_freq)
    ang = jnp.concatenate([row[:, None] * inv, colp[:, None] * inv], axis=-1)
    cos2 = jnp.concatenate([jnp.cos(ang), jnp.cos(ang)], axis=-1)
    sin2 = jnp.concatenate([-jnp.sin(ang), jnp.sin(ang)], axis=-1)

    xl = x.reshape(n_lat, d)
    xc = ctx.reshape(n_ctx, d)
    yl = yc = None
    ctx_rb = n_lat // c_len

    for i in range(depth):
        last = i == depth - 1
        prev_gate = (i - 1, 5)
        g1 = norm1_g[i][None, :]
        xl, hl = resnorm(xl, yl, mods3, g1, seg_fn=seg_lat, gate=prev_gate, sc=(i, 1), sh=(i, 0), out_dtype=BF16, tm=512)
        xc, hc = resnorm(xc, yc, mods3, g1, seg_fn=seg_ctx, gate=prev_gate, sc=(i, 1), sh=(i, 0), out_dtype=BF16, tm=512)
        h = jnp.concatenate([hl, hc], axis=0)
        tm_all = h.shape[0] // 8

        pa = matmul_nt(h, w_in_t, i, row0=0, n_cols=qkvgz_xbc, tn=1024, tm=tm_all, out_dtype=BF16)
        pb = matmul_nt(h, w_in_t, i, row0=tail0, n_cols=n_tail, tn=1024, tm=tm_all, out_dtype=BF16)
        pdt = matmul_nt(h, w_in_t, i, row0=qkvgz_xbc, n_cols=128, tn=128, tm=tm_all, out_dtype=F32)

        xbc_l = dwconv(pa, ssd_conv_w, ssd_conv_b, i, n=n, nb=nb, row_block0=0, col0=8192, n_cols=3072, act=True, out_dtype=BF16)
        xbc_c = dwconv(pa, ssd_conv_w, ssd_conv_b, i, n=c_len, nb=nb, row_block0=ctx_rb, col0=8192, n_cols=3072, act=True, out_dtype=BF16)
        xr_l = dwconv(pb, lru_conv_w, lru_conv_b, i, n=n, nb=nb, row_block0=0, col0=2048, n_cols=2048, act=False, out_dtype=F32)
        xr_c = dwconv(pb, lru_conv_w, lru_conv_b, i, n=c_len, nb=nb, row_block0=ctx_rb, col0=2048, n_cols=2048, act=False, out_dtype=F32)

        ret_l, ret_c = retention(pa, cos2, sin2, log_g[i], ret_gn_g, i, nb=nb, n=n, c=c_len)
        ssd_l, ssd_c = ssd(xbc_l, xbc_c, pdt, pa, dtb, aneg, dskip, i, nb=nb, n=n, c=c_len)
        lru_l, lru_c = rglru(xr_l, xr_c, pb, wcat, gate_b, lru_lambda, i, nb=nb, n=n, c=c_len)
        sg = ssd_norm_g[i][None, :]

        dense = i % 2 == 0
        h2_dtype = BF16 if dense else F32

        def mixer_out(xres, u_ret, y_ssd, u_lru, seg_fn, row_block0, tm):
            _, u_ssd = resnorm(y_ssd, None, mods3, sg, seg_fn=seg_fn, gate=None, sc=None, sh=None, out_dtype=BF16, tm=512)
            merged = branch_merge(u_ret, u_ssd, u_lru, pb, w_br, i, row_block0=row_block0, merge_col0=4096, tm=tm, tn=512)
            y = matmul(merged, w_out, (i,), col_block0=0, n_cols=d, tn=1024, tm=tm, out_dtype=BF16)
            return resnorm(xres, y, mods3, norm2_g[i][None, :], seg_fn=seg_fn, gate=(i, 2), sc=(i, 4), sh=(i, 3),
                           out_dtype=h2_dtype, tm=512)

        xl, h2l = mixer_out(xl, ret_l, ssd_l, lru_l, seg_lat, 0, 1024)
        if not last:
            xc, h2c = mixer_out(xc, ret_c, ssd_c, lru_c, seg_ctx, n_lat // 512, 512)
        j = i // 2
        if dense:
            yl = ffn_dense(h2l, ffn_w1, ffn_w3, ffn_w2, j, tm=1024, tf=256)
            if not last:
                yc = ffn_dense(h2c, ffn_w1, ffn_w3, ffn_w2, j, tm=512, tf=256)
        else:
            h2 = h2l if last else jnp.concatenate([h2l, h2c], axis=0)
            route = router(h2, router_wp, router_bp, j, tm=512)
            y2 = ffn_experts(h2, route, moe_w1, moe_w3, moe_w2, j, tm=512, tf=512)
            mb = h2.shape[0] // 512
            yl = [(y2, 0), (y2, mb)]
            yc = [(y2, n_lat // 512), (y2, mb + n_lat // 512)]

    _, out = resnorm(xl, yl, mods3, final_g[None, :], seg_fn=seg_lat, gate=(depth - 1, 5), sc=None, sh=None, out_dtype=F32, tm=512)
    return out.reshape(nb, n, d)
```

```python
import functools

import jax
import jax.numpy as jnp
from jax import lax
from jax.experimental import pallas as pl
from jax.experimental.pallas import tpu as pltpu

F32 = jnp.float32
BF16 = jnp.bfloat16

EPS = 1e-6
CHUNK = 128
CONV_W = 4
RET_HEADS = 8
RET_DK = 128
RET_DV = 256
SSD_HEADS = 32
SSD_HEADDIM = 64
SSD_GROUPS = 4
SSD_STATE = 128
LRU_BW = 128
LRU_C = 8.0
N_EXPERTS = 8
ROPE_BASE = 10000.0
GRID_W = 64

MIB = 1024 * 1024

_NT = (((1,), (1,)), ((), ()))
_TN = (((0,), (0,)), ((), ()))


def _params(n_axes, vmem_mib):
    return pltpu.CompilerParams(dimension_semantics=("arbitrary",) * n_axes,
                                vmem_limit_bytes=int(vmem_mib * MIB))


def _silu(x):
    return x * jax.nn.sigmoid(x)


def _dot(a, b):
    return jnp.dot(a, b, preferred_element_type=F32)


def _mod_kernel(c_ref, w_ref, b_ref, o_ref):
    x = _silu(c_ref[...]).astype(BF16)
    o_ref[...] = _dot(x, w_ref[...].astype(BF16)) + b_ref[...]


def modulation(c8, w_mod, b_mod):
    depth, d, n6 = w_mod.shape
    tn = 1536
    return pl.pallas_call(
        _mod_kernel,
        grid=(depth, n6 // tn),
        in_specs=[pl.BlockSpec((8, d), lambda l, j: (0, 0)),
                  pl.BlockSpec((None, d, tn), lambda l, j: (l, 0, j)),
                  pl.BlockSpec((None, 1, tn), lambda l, j: (l, 0, j))],
        out_specs=pl.BlockSpec((None, 8, tn), lambda l, j: (l, 0, j)),
        out_shape=jax.ShapeDtypeStruct((depth, 8, n6), F32),
        compiler_params=_params(2, 48),
        name="modulation",
    )(c8, w_mod, b_mod.reshape(depth, 1, n6))


def _resnorm_kernel(*refs, n_y, has_mod):
    refs = list(refs)
    x_ref = refs.pop(0)
    has_y = n_y > 0
    if has_y:
        y_refs = [refs.pop(0) for _ in range(n_y)]
        gate_ref = refs.pop(0)
    g_ref = refs.pop(0)
    if has_mod:
        sc_ref, sh_ref = refs.pop(0), refs.pop(0)
    x = x_ref[...].astype(F32)
    if has_y:
        xo_ref = refs.pop(0)
        y = y_refs[0][...].astype(F32)
        for y_ref in y_refs[1:]:
            y = y + y_ref[...].astype(F32)
        x = x + gate_ref[...] * y
        xo_ref[...] = x
    h_ref = refs.pop(0)
    ms = jnp.mean(x * x, axis=-1, keepdims=True)
    yn = x * lax.rsqrt(ms + EPS) * g_ref[...]
    if has_mod:
        yn = yn * (1.0 + sc_ref[...]) + sh_ref[...]
    h_ref[...] = yn.astype(h_ref.dtype)


def resnorm(x, y, mods3, norm_g, *, seg_fn, gate, sc, sh, out_dtype, tm):
    m, d = x.shape
    if y is None:
        ys = []
    elif isinstance(y, list):
        ys = y
    else:
        ys = [(y, 0)]
    has_y = bool(ys)
    has_mod = sc is not None

    def mod_spec(layer_chunk):
        layer, chunk = layer_chunk
        return pl.BlockSpec((None, 1, d), lambda i: ((layer * 8 + seg_fn(i)) * 6 + chunk, 0, 0))

    row = pl.BlockSpec((tm, d), lambda i: (i, 0))
    in_specs, args = [row], [x]
    if has_y:
        for arr, off in ys:
            in_specs.append(pl.BlockSpec((tm, d), lambda i, off=off: (off + i, 0)))
            args.append(arr)
        in_specs.append(mod_spec(gate))
        args.append(mods3)
    in_specs.append(pl.BlockSpec((1, d), lambda i: (0, 0)))
    args.append(norm_g)
    if has_mod:
        in_specs += [mod_spec(sc), mod_spec(sh)]
        args += [mods3, mods3]
    out_specs, out_shape = [], []
    if has_y:
        out_specs.append(row)
        out_shape.append(jax.ShapeDtypeStruct((m, d), F32))
    out_specs.append(row)
    out_shape.append(jax.ShapeDtypeStruct((m, d), out_dtype))
    res = pl.pallas_call(
        functools.partial(_resnorm_kernel, n_y=len(ys), has_mod=has_mod),
        grid=(m // tm,), in_specs=in_specs, out_specs=out_specs, out_shape=out_shape,
        compiler_params=_params(1, 48), name="resnorm",
    )(*args)
    return (res[0], res[1]) if has_y else (x, res[0])


def _mm_kernel(x_ref, w_ref, o_ref, *scratch):
    if scratch:
        wb = scratch[0]

        @pl.when(pl.program_id(1) == 0)
        def _():
            wb[...] = w_ref[...].astype(BF16)

        w = wb[...]
    else:
        w = w_ref[...]
    o_ref[...] = _dot(x_ref[...], w).astype(o_ref.dtype)


def matmul(x, w, prefix, *, col_block0, n_cols, tn, tm, out_dtype, m_rows=None):
    m, k = x.shape
    m = m_rows or m
    cast = w.dtype != BF16
    lead = (None,) * len(prefix)
    return pl.pallas_call(
        _mm_kernel,
        grid=(n_cols // tn, m // tm),
        in_specs=[pl.BlockSpec((tm, k), lambda j, i: (i, 0)),
                  pl.BlockSpec(lead + (k, tn), lambda j, i: tuple(prefix) + (0, col_block0 + j))],
        out_specs=pl.BlockSpec((tm, tn), lambda j, i: (i, j)),
        out_shape=jax.ShapeDtypeStruct((m, n_cols), out_dtype),
        scratch_shapes=[pltpu.VMEM((k, tn), BF16)] if cast else [],
        compiler_params=_params(2, 56), name="matmul",
    )(x, w)


def _mm_nt_kernel(x_ref, w_ref, o_ref, wb):
    @pl.when(pl.program_id(1) == 0)
    def _():
        wb[...] = w_ref[...].astype(BF16)

    o_ref[...] = lax.dot_general(x_ref[...], wb[...], _NT, preferred_element_type=F32).astype(o_ref.dtype)


def matmul_nt(x, w_t, layer, *, row0, n_cols, tn, tm, out_dtype):
    m, k = x.shape
    return pl.pallas_call(
        _mm_nt_kernel,
        grid=(n_cols // tn, m // tm),
        in_specs=[pl.BlockSpec((tm, k), lambda j, i: (i, 0)),
                  pl.BlockSpec((None, pl.Element(tn), pl.Element(k)),
                               lambda j, i: (layer, pl.multiple_of(row0 + j * tn, 8), 0))],
        out_specs=pl.BlockSpec((tm, tn), lambda j, i: (i, j)),
        out_shape=jax.ShapeDtypeStruct((m, n_cols), out_dtype),
        scratch_shapes=[pltpu.VMEM((tn, k), BF16)],
        compiler_params=_params(2, 56), name="matmul_nt",
    )(x, w_t)


def _conv_kernel(x_ref, w_ref, b_ref, o_ref, scr, *, n, act, rc):
    tc = x_ref.shape[1]
    scr[0:8, :] = jnp.zeros((8, tc), F32)
    scr[8 + n:16 + n, :] = jnp.zeros((8, tc), F32)

    def copy_in(c, carry):
        r0 = pl.multiple_of(c * rc, rc)
        scr[pl.ds(8 + r0, rc), :] = x_ref[pl.ds(r0, rc), :].astype(F32)
        return carry

    lax.fori_loop(0, n // rc, copy_in, 0)
    w = w_ref[...]
    b = b_ref[...]

    def conv(c, carry):
        r0 = pl.multiple_of(c * rc, rc)
        win = scr[pl.ds(r0, rc + 16), :]
        acc = b + w[2:3] * win[8:8 + rc]
        for j in (0, 1, 3):
            rolled = pltpu.roll(win, (2 - j) % (rc + 16), 0)
            acc = acc + w[j:j + 1] * rolled[8:8 + rc]
        if act:
            acc = _silu(acc)
        o_ref[pl.ds(r0, rc), :] = acc.astype(o_ref.dtype)
        return carry

    lax.fori_loop(0, n // rc, conv, 0)


def dwconv(p, conv_w, conv_b, layer, *, n, nb, row_block0, col0, n_cols, act, out_dtype):
    tc = 512
    rc = 64
    depth = conv_w.shape[0]
    return pl.pallas_call(
        functools.partial(_conv_kernel, n=n, act=act, rc=rc),
        grid=(nb, n_cols // tc),
        in_specs=[pl.BlockSpec((n, tc), lambda b, j: (row_block0 + b, col0 // tc + j)),
                  pl.BlockSpec((None, CONV_W, tc), lambda b, j: (layer, 0, j)),
                  pl.BlockSpec((None, 1, tc), lambda b, j: (layer, 0, j))],
        out_specs=pl.BlockSpec((n, tc), lambda b, j: (b, j)),
        out_shape=jax.ShapeDtypeStruct((nb * n, n_cols), out_dtype),
        scratch_shapes=[pltpu.VMEM((n + 16, tc), F32)],
        compiler_params=_params(2, 48), name="dwconv",
    )(p, conv_w, conv_b.reshape(depth, 1, -1))


def _ret_kernel(lg_ref, ql, kl, vl, gl, qc, kc, vc, gc, cos_ref, sin_ref, gn_ref,
                ul_ref, uc_ref, qs, ks, acc, *, n, c):
    h = pl.program_id(1)
    lgf = lg_ref[0, h]
    lgb = lg_ref[1, h]
    ii = lax.broadcasted_iota(jnp.int32, (CHUNK, CHUNK), 0).astype(F32)
    jj = lax.broadcasted_iota(jnp.int32, (CHUNK, CHUNK), 1).astype(F32)
    diff = ii - jj
    dsum = (jnp.where(diff >= 0, jnp.exp(lgf * jnp.maximum(diff, 0.0)), 0.0)
            + jnp.where(diff <= 0, jnp.exp(lgb * jnp.maximum(-diff, 0.0)), 0.0))
    col = lax.broadcasted_iota(jnp.int32, (CHUNK, 1), 0).astype(F32)
    kdec_f = jnp.exp(lgf * (CHUNK - 1.0 - col))
    kdec_b = jnp.exp(lgb * col)
    qdec_f = jnp.exp(lgf * (col + 1.0))
    qdec_b = jnp.exp(lgb * (CHUNK - col))
    one = jnp.ones((1, 1), F32)
    cd_f = jnp.exp(one * (lgf * CHUNK))
    cd_b = jnp.exp(one * (lgb * CHUNK))
    scale = RET_DK ** -0.5

    def fwd_chunk(r0, q, k, v, s_in):
        qs[pl.ds(r0, CHUNK), :] = q
        ks[pl.ds(r0, CHUNK), :] = k
        s = lax.dot_general(q, k, _NT, preferred_element_type=F32)
        y = _dot((s * dsum).astype(BF16), v)
        y = y + _dot((q.astype(F32) * qdec_f).astype(BF16), s_in.astype(BF16))
        kv = lax.dot_general((k.astype(F32) * kdec_f).astype(BF16), v, _TN, preferred_element_type=F32)
        acc[pl.ds(r0, CHUNK), :] = y
        return cd_f * s_in + kv

    def bwd_chunk(r0, v, s_in):
        q = qs[pl.ds(r0, CHUNK), :]
        k = ks[pl.ds(r0, CHUNK), :]
        y = _dot((q.astype(F32) * qdec_b).astype(BF16), s_in.astype(BF16))
        acc[pl.ds(r0, CHUNK), :] = acc[pl.ds(r0, CHUNK), :] + y
        kv = lax.dot_general((k.astype(F32) * kdec_b).astype(BF16), v, _TN, preferred_element_type=F32)
        return cd_b * s_in + kv

    def rot(t_ref, r):
        t = t_ref[pl.ds(r, CHUNK), :].astype(F32)
        return t * cos_ref[pl.ds(r, CHUNK), :] + pltpu.roll(t, RET_DK // 2, 1) * sin_ref[pl.ds(r, CHUNK), :]

    s_f = jnp.zeros((RET_DK, RET_DV), F32)
    for ci in range(c // CHUNK):
        r = ci * CHUNK
        q = qc[r:r + CHUNK, :]
        k = (kc[r:r + CHUNK, :].astype(F32) * scale).astype(BF16)
        s_f = fwd_chunk(r, q, k, vc[r:r + CHUNK, :], s_f)

    def fwd_lat(i, s_in):
        r = pl.multiple_of(i * CHUNK, CHUNK)
        q = rot(ql, r).astype(BF16)
        k = (rot(kl, r) * scale).astype(BF16)
        return fwd_chunk(c + r, q, k, vl[pl.ds(r, CHUNK), :], s_in)

    lax.fori_loop(0, n // CHUNK, fwd_lat, s_f, unroll=4)

    s_b = jnp.zeros((RET_DK, RET_DV), F32)
    for ci in reversed(range(c // CHUNK)):
        r = ci * CHUNK
        s_b = bwd_chunk(r, vc[r:r + CHUNK, :], s_b)

    def bwd_lat(i, s_in):
        r = pl.multiple_of(n - CHUNK - i * CHUNK, CHUNK)
        return bwd_chunk(c + r, vl[pl.ds(r, CHUNK), :], s_in)

    lax.fori_loop(0, n // CHUNK, bwd_lat, s_b, unroll=4)

    gn = gn_ref[...]

    def finish(y, g):
        mu = jnp.mean(y, axis=-1, keepdims=True)
        yc = y - mu
        var = jnp.mean(yc * yc, axis=-1, keepdims=True)
        return (_silu(g.astype(F32)) * (yc * lax.rsqrt(var + EPS) * gn)).astype(BF16)

    for ci in range(c // CHUNK):
        r = ci * CHUNK
        uc_ref[r:r + CHUNK, :] = finish(acc[r:r + CHUNK, :], gc[r:r + CHUNK, :])

    def fin_lat(i, carry):
        r = pl.multiple_of(i * CHUNK, CHUNK)
        ul_ref[pl.ds(r, CHUNK), :] = finish(acc[pl.ds(c + r, CHUNK), :], gl[pl.ds(r, CHUNK), :])
        return carry

    lax.fori_loop(0, n // CHUNK, fin_lat, 0, unroll=4)


def retention(pa, cos2, sin2, log_g, gn_g, layer, *, nb, n, c):
    rb = (nb * n) // c
    lat = lambda w, off: pl.BlockSpec((n, w), lambda b, h: (b, off + h))
    ctx = lambda w, off: pl.BlockSpec((c, w), lambda b, h: (rb + b, off + h))
    full = lambda: pl.BlockSpec((n, RET_DK), lambda b, h: (0, 0))
    depth = gn_g.shape[0]
    return pl.pallas_call(
        functools.partial(_ret_kernel, n=n, c=c),
        grid=(nb, RET_HEADS),
        in_specs=[pl.BlockSpec(memory_space=pltpu.SMEM),
                  lat(RET_DK, 0), lat(RET_DK, 8), lat(RET_DV, 8), lat(RET_DV, 16),
                  ctx(RET_DK, 0), ctx(RET_DK, 8), ctx(RET_DV, 8), ctx(RET_DV, 16),
                  full(), full(),
                  pl.BlockSpec((None, 1, RET_DV), lambda b, h: (layer, 0, h))],
        out_specs=[pl.BlockSpec((n, RET_DV), lambda b, h: (b, h)),
                   pl.BlockSpec((c, RET_DV), lambda b, h: (b, h))],
        out_shape=[jax.ShapeDtypeStruct((nb * n, RET_HEADS * RET_DV), BF16),
                   jax.ShapeDtypeStruct((nb * c, RET_HEADS * RET_DV), BF16)],
        scratch_shapes=[pltpu.VMEM((c + n, RET_DK), BF16), pltpu.VMEM((c + n, RET_DK), BF16),
                        pltpu.VMEM((c + n, RET_DV), F32)],
        compiler_params=_params(2, 48), name="retention",
    )(log_g, pa, pa, pa, pa, pa, pa, pa, pa, cos2, sin2, gn_g.reshape(depth, 1, -1))


def _split3(x):
    hi = x.astype(BF16)
    r = x - hi.astype(F32)
    mid = r.astype(BF16)
    lo = (r - mid.astype(F32)).astype(BF16)
    return hi, mid, lo


def _dot_split_l(x, m, terms):
    parts = _split3(x)[:terms]
    out = _dot(parts[0], m)
    for p in parts[1:]:
        out = out + _dot(p, m)
    return out


def _dot_split_r(m, x, terms):
    parts = _split3(x)[:terms]
    out = _dot(m, parts[0])
    for p in parts[1:]:
        out = out + _dot(m, p)
    return out


def _ssd_kernel(xl, bl, cl, dtl, zl, xc, bc, cc, dtc, zc, dtb_ref, an_ref, dsk_ref,
                yl_ref, yc_ref, acc, s_f, s_b, pt_f, pt_b, xw_b, ex_b, tt_b, *, n, c):
    g = pl.program_id(1)
    L = CHUNK
    hg = SSD_HEADS // SSD_GROUPS
    gw = hg * SSD_HEADDIM
    ii = lax.broadcasted_iota(jnp.int32, (L, L), 0)
    jj = lax.broadcasted_iota(jnp.int32, (L, L), 1)
    causal = ii >= jj
    anti = ii <= jj
    tril = causal.astype(BF16)
    triu = anti.astype(BF16)
    kk = lax.broadcasted_iota(jnp.int32, (L, gw), 0)
    mm = lax.shift_right_logical(lax.broadcasted_iota(jnp.int32, (L, gw), 1), 6)
    e_f = (kk == hg * g + mm).astype(BF16)
    e_b = (kk == SSD_HEADS + hg * g + mm).astype(BF16)
    lane2 = lax.broadcasted_iota(jnp.int32, (L, 2 * SSD_HEADDIM), 1)
    left = lane2 < SSD_HEADDIM
    dtb = dtb_ref[...]
    an = an_ref[...]
    dsk = dsk_ref[...]

    def fwd_chunk(r0, x, bm, cm, dt_raw):
        xf = x.astype(F32)
        dt = jax.nn.softplus(dt_raw + dtb)
        a = dt * an
        psm_f = _dot_split_r(tril, a, 2)
        psm_b = psm_f[L - 1:L, :] - psm_f + a
        pexp_f = _dot_split_l(psm_f, e_f, 2)
        pexp_b = _dot_split_l(psm_b, e_b, 2)
        dtb16 = dt.astype(BF16)
        pt_f[...] = psm_f.T
        pt_b[...] = psm_b.T
        xdt_f = xf * _dot(dtb16, e_f)
        xdt_b = xf * _dot(dtb16, e_b)
        tot_b = pexp_b[0:1, :]
        xw_b[pl.ds(r0, L), :] = (xdt_b * jnp.exp(tot_b - pexp_b)).astype(BF16)
        ex_b[pl.ds(r0, L), :] = jnp.exp(pexp_b).astype(BF16)
        tt_b[pl.ds(r0 // L, 1)] = jnp.broadcast_to(jnp.exp(tot_b), (1, 8, gw))
        xcat = jnp.concatenate([xdt_f.astype(BF16), xdt_b.astype(BF16)], axis=0)
        cb = lax.dot_general(cm, bm, _NT, preferred_element_type=F32)
        for jp in range(hg // 2):
            lo = jp * 2 * SSD_HEADDIM
            xpair = xcat[:, lo:lo + 2 * SSD_HEADDIM]
            ys = []
            for j in (2 * jp, 2 * jp + 1):
                col = j * SSD_HEADDIM
                seg_f = pexp_f[:, col:col + 1] - pt_f[pl.ds(hg * g + j, 1), :]
                seg_b = pexp_b[:, col:col + 1] - pt_b[pl.ds(SSD_HEADS + hg * g + j, 1), :]
                m_f = cb * jnp.exp(jnp.where(causal, seg_f, -jnp.inf))
                m_b = cb * jnp.exp(jnp.where(anti, seg_b, -jnp.inf))
                mcat = jnp.concatenate([m_f.astype(BF16), m_b.astype(BF16)], axis=1)
                ys.append(_dot(mcat, xpair))
            acc[pl.ds(r0, L), lo:lo + 2 * SSD_HEADDIM] = jnp.where(left, ys[0], ys[1])
        s_in = s_f[...]
        tot = pexp_f[L - 1:L, :]
        y_off = _dot(cm, s_in.astype(BF16)) * jnp.exp(pexp_f)
        st = lax.dot_general(bm, (xdt_f * jnp.exp(tot - pexp_f)).astype(BF16), _TN, preferred_element_type=F32)
        s_f[...] = jnp.exp(tot) * s_in + st
        acc[pl.ds(r0, L), :] = acc[pl.ds(r0, L), :] + y_off + dsk * xf

    def bwd_chunk(r0, bm, cm):
        s_in = s_b[...]
        y_off = _dot(cm, s_in.astype(BF16)) * ex_b[pl.ds(r0, L), :].astype(F32)
        st = lax.dot_general(bm, xw_b[pl.ds(r0, L), :], _TN, preferred_element_type=F32)
        s_b[...] = tt_b[r0 // L][0:1, :] * s_in + st
        acc[pl.ds(r0, L), :] = acc[pl.ds(r0, L), :] + y_off

    s_f[...] = jnp.zeros_like(s_f)
    s_b[...] = jnp.zeros_like(s_b)
    nc_ctx = c // L
    for ci in range(nc_ctx):
        r = ci * L
        fwd_chunk(r, xc[r:r + L, :], bc[r:r + L, :], cc[r:r + L, :], dtc[r:r + L, :])

    def fwd_lat(i, carry):
        r = pl.multiple_of(i * L, L)
        fwd_chunk(c + r, xl[pl.ds(r, L), :], bl[pl.ds(r, L), :], cl[pl.ds(r, L), :], dtl[pl.ds(r, L), :])
        return carry

    lax.fori_loop(0, n // L, fwd_lat, 0, unroll=2)

    for ci in reversed(range(nc_ctx)):
        r = ci * L
        bwd_chunk(r, bc[r:r + L, :], cc[r:r + L, :])

    def bwd_lat(i, carry):
        r = pl.multiple_of(n - L - i * L, L)
        bwd_chunk(c + r, bl[pl.ds(r, L), :], cl[pl.ds(r, L), :])
        return carry

    lax.fori_loop(0, n // L, bwd_lat, 0, unroll=4)

    for ci in range(nc_ctx):
        r = ci * L
        yc_ref[r:r + L, :] = (acc[r:r + L, :] * _silu(zc[r:r + L, :].astype(F32))).astype(BF16)

    def fin_lat(i, carry):
        r = pl.multiple_of(i * L, L)
        yl_ref[pl.ds(r, L), :] = (acc[pl.ds(c + r, L), :] * _silu(zl[pl.ds(r, L), :].astype(F32))).astype(BF16)
        return carry

    lax.fori_loop(0, n // L, fin_lat, 0)


def ssd(xbc_l, xbc_c, pdt, pa, dtb, aneg, dskip, layer, *, nb, n, c):
    hg = SSD_HEADS // SSD_GROUPS
    gw = hg * SSD_HEADDIM
    rb = (nb * n) // c
    inner = SSD_HEADS * SSD_HEADDIM
    xoff = 0
    boff = inner // SSD_STATE
    coff = boff + SSD_GROUPS
    zoff = 6144 // gw
    lat = lambda w, off: pl.BlockSpec((n, w), lambda b, g: (b, off + g))
    ctx = lambda w, off: pl.BlockSpec((c, w), lambda b, g: (b, off + g))
    ctxp = lambda w, off: pl.BlockSpec((c, w), lambda b, g: (rb + b, off + g))
    vec = lambda w: pl.BlockSpec((None, 1, w), lambda b, g: (layer, 0, 0))
    return pl.pallas_call(
        functools.partial(_ssd_kernel, n=n, c=c),
        grid=(nb, SSD_GROUPS),
        in_specs=[lat(gw, xoff), lat(SSD_STATE, boff), lat(SSD_STATE, coff),
                  pl.BlockSpec((n, 128), lambda b, g: (b, 0)), lat(gw, zoff),
                  ctx(gw, xoff), ctx(SSD_STATE, boff), ctx(SSD_STATE, coff),
                  pl.BlockSpec((c, 128), lambda b, g: (rb + b, 0)), ctxp(gw, zoff),
                  vec(128), vec(128),
                  pl.BlockSpec((None, 1, gw), lambda b, g: (layer, 0, g))],
        out_specs=[pl.BlockSpec((n, gw), lambda b, g: (b, g)),
                   pl.BlockSpec((c, gw), lambda b, g: (b, g))],
        out_shape=[jax.ShapeDtypeStruct((nb * n, inner), BF16),
                   jax.ShapeDtypeStruct((nb * c, inner), BF16)],
        scratch_shapes=[pltpu.VMEM((c + n, gw), F32),
                        pltpu.VMEM((SSD_STATE, gw), F32), pltpu.VMEM((SSD_STATE, gw), F32),
                        pltpu.VMEM((128, CHUNK), F32), pltpu.VMEM((128, CHUNK), F32),
                        pltpu.VMEM((c + n, gw), BF16), pltpu.VMEM((c + n, gw), BF16),
                        pltpu.VMEM(((c + n) // CHUNK, 8, gw), F32)],
        compiler_params=_params(2, 60), name="ssd",
    )(xbc_l, xbc_l, xbc_l, pdt, pa, xbc_c, xbc_c, xbc_c, pdt, pa, dtb, aneg, dskip)


def _gelu_tanh(x):
    return 0.5 * x * (1.0 + jnp.tanh(0.7978845608028654 * (x + 0.044715 * (x * x * x))))


LRU_SEGS = 8


def _lru_kernel(xl, gl, xc, gc, w_ref, b_ref, lam_ref, ul_ref, uc_ref,
                ca_f, ch_f, ca_b, ch_b, p_f, h_f, p_b, h_b, cin, *, n, c, rc):
    seg = n // LRU_SEGS
    nsb = xl.shape[1] // LRU_BW
    sp = jax.nn.softplus(-lam_ref[...])
    bias = b_ref[...]

    def gates(x_ref, r0, store):
        x = x_ref[pl.ds(r0, rc), :]
        xb = x.astype(BF16)
        for sb in range(nsb):
            lanes = slice(sb * LRU_BW, (sb + 1) * LRU_BW)
            pre = _dot(xb[:, lanes], w_ref[sb])
            for d in range(2):
                r = jax.nn.sigmoid(pre[:, (2 * d) * LRU_BW:(2 * d + 1) * LRU_BW] + bias[2 * d:2 * d + 1, lanes])
                i = jax.nn.sigmoid(pre[:, (2 * d + 1) * LRU_BW:(2 * d + 2) * LRU_BW] + bias[2 * d + 1:2 * d + 2, lanes])
                log_a = -LRU_C * r * sp[d:d + 1, lanes]
                a = jnp.exp(log_a)
                mult = jnp.sqrt(jnp.maximum(1.0 - a * a, 0.0))
                store(d, sb, a, mult * (i * x[:, lanes]))

    def store_ctx(r0):
        def store(d, sb, a, bx):
            lanes = slice(sb * LRU_BW, (sb + 1) * LRU_BW)
            (ca_f, ca_b)[d][pl.ds(r0, rc), lanes] = a
            (ch_f, ch_b)[d][pl.ds(r0, rc), lanes] = bx
        return store

    for ci in range(c // rc):
        gates(xc, ci * rc, store_ctx(ci * rc))

    def ctx_step(i, carry):
        hf, hb = carry
        t_b = c - 1 - i
        hf = ca_f[pl.ds(i, 1), :] * hf + ch_f[pl.ds(i, 1), :]
        ch_f[pl.ds(i, 1), :] = hf
        hb = ca_b[pl.ds(t_b, 1), :] * hb + ch_b[pl.ds(t_b, 1), :]
        ch_b[pl.ds(t_b, 1), :] = hb
        return hf, hb

    zero = jnp.zeros((1, xl.shape[1]), F32)
    hf_ctx, hb_ctx = lax.fori_loop(0, c, ctx_step, (zero, zero), unroll=8)

    for ci in range(c // rc):
        r = ci * rc
        uc_ref[r:r + rc, :] = ((ch_f[r:r + rc, :] + ch_b[r:r + rc, :]) * _gelu_tanh(gc[r:r + rc, :].astype(F32))).astype(BF16)

    def gates_lat(ci, carry):
        r0 = pl.multiple_of(ci * rc, rc)
        s = r0 // seg
        row0 = (r0 - s * seg) * LRU_SEGS + s

        def store(d, sb, a, bx):
            (p_f, p_b)[d][sb, pl.ds(row0, rc, stride=LRU_SEGS), :] = a
            (h_f, h_b)[d][sb, pl.ds(row0, rc, stride=LRU_SEGS), :] = bx

        gates(xl, r0, store)
        return carry

    lax.fori_loop(0, n // rc, gates_lat, 0, unroll=2)

    def lat_step(i, carry):
        out = []
        for d, (p_s, h_s) in enumerate(((p_f, h_f), (p_b, h_b))):
            tt = i if d == 0 else seg - 1 - i
            row = pl.multiple_of(tt * LRU_SEGS, LRU_SEGS)
            for sb in range(nsb):
                h, p = carry[d * nsb + sb]
                a = p_s[sb, pl.ds(row, LRU_SEGS), :]
                h = a * h + h_s[sb, pl.ds(row, LRU_SEGS), :]
                p = a * p
                h_s[sb, pl.ds(row, LRU_SEGS), :] = h
                p_s[sb, pl.ds(row, LRU_SEGS), :] = p
                out.append((h, p))
        return tuple(out)

    tile0 = (jnp.zeros((LRU_SEGS, LRU_BW), F32), jnp.ones((LRU_SEGS, LRU_BW), F32))
    ends = lax.fori_loop(0, seg, lat_step, (tile0,) * (2 * nsb), unroll=4)

    sub = lax.broadcasted_iota(jnp.int32, (LRU_SEGS, LRU_BW), 0)
    for d in range(2):
        for sb in range(nsb):
            lanes = slice(sb * LRU_BW, (sb + 1) * LRU_BW)
            h_end, p_end = ends[d * nsb + sb]
            cur = jnp.broadcast_to((hf_ctx, hb_ctx)[d][:, lanes], (LRU_SEGS, LRU_BW))
            cmat = jnp.zeros((LRU_SEGS, LRU_BW), F32)
            order = range(LRU_SEGS) if d == 0 else reversed(range(LRU_SEGS))
            for s in order:
                cmat = jnp.where(sub == s, cur, cmat)
                nxt = h_end + p_end * cur
                cur = jnp.broadcast_to(nxt[s:s + 1, :], (LRU_SEGS, LRU_BW))
            cin[d * nsb + sb] = cmat

    def fin_lat(ci, carry):
        r0 = pl.multiple_of(ci * rc, rc)
        s = r0 // seg
        row0 = (r0 - s * seg) * LRU_SEGS + s
        g = _gelu_tanh(gl[pl.ds(r0, rc), :].astype(F32))
        for sb in range(nsb):
            lanes = slice(sb * LRU_BW, (sb + 1) * LRU_BW)
            y = None
            for d, (p_s, h_s) in enumerate(((p_f, h_f), (p_b, h_b))):
                rows = pl.ds(row0, rc, stride=LRU_SEGS)
                t = h_s[sb, rows, :] + p_s[sb, rows, :] * cin[d * nsb + sb, pl.ds(s, 1), :]
                y = t if y is None else y + t
            ul_ref[pl.ds(r0, rc), lanes] = (y * g[:, lanes]).astype(BF16)
        return carry

    lax.fori_loop(0, n // rc, fin_lat, 0)


def rglru(xr_l, xr_c, pb, wcat, gate_b, lam, layer, *, nb, n, c):
    tcw = 2 * LRU_BW
    width = xr_l.shape[1]
    rb = (nb * n) // c
    return pl.pallas_call(
        functools.partial(_lru_kernel, n=n, c=c, rc=min(128, n // LRU_SEGS)),
        grid=(nb, width // tcw),
        in_specs=[pl.BlockSpec((n, tcw), lambda b, j: (b, j)),
                  pl.BlockSpec((n, tcw), lambda b, j: (b, j)),
                  pl.BlockSpec((c, tcw), lambda b, j: (b, j)),
                  pl.BlockSpec((c, tcw), lambda b, j: (rb + b, j)),
                  pl.BlockSpec((None, 2, LRU_BW, 4 * LRU_BW), lambda b, j: (layer, j, 0, 0)),
                  pl.BlockSpec((None, 4, tcw), lambda b, j: (layer, 0, j)),
                  pl.BlockSpec((None, 2, tcw), lambda b, j: (layer, 0, j))],
        out_specs=[pl.BlockSpec((n, tcw), lambda b, j: (b, j)),
                   pl.BlockSpec((c, tcw), lambda b, j: (b, j))],
        out_shape=[jax.ShapeDtypeStruct((nb * n, width), BF16),
                   jax.ShapeDtypeStruct((nb * c, width), BF16)],
        scratch_shapes=([pltpu.VMEM((c, tcw), F32)] * 4
                        + [pltpu.VMEM((tcw // LRU_BW, n, LRU_BW), F32)] * 4
                        + [pltpu.VMEM((2 * tcw // LRU_BW, LRU_SEGS, LRU_BW), F32)]),
        compiler_params=_params(2, 56), name="rglru",
    )(xr_l, pb, xr_c, pb, wcat, gate_b, lam)


def _branch_kernel(u0, u1, u2, m0, m1, m2, w_ref, o_ref):
    acc = None
    for k, (u, mr) in enumerate(((u0, m0), (u1, m1), (u2, m2))):
        t = jax.nn.sigmoid(mr[...].astype(F32)) * _dot(u[...], w_ref[k])
        acc = t if acc is None else acc + t
    o_ref[...] = acc.astype(o_ref.dtype)


def branch_merge(u_ret, u_ssd, u_lru, pb, w_branch, layer, *, row_block0, merge_col0, tm, tn):
    m, d = u_ret.shape
    u_spec = pl.BlockSpec((tm, d), lambda j, i: (i, 0))
    m_spec = lambda k: pl.BlockSpec((tm, tn), lambda j, i: (row_block0 + i, (merge_col0 + k * d) // tn + j))
    return pl.pallas_call(
        _branch_kernel,
        grid=(d // tn, m // tm),
        in_specs=[u_spec, u_spec, u_spec, m_spec(0), m_spec(1), m_spec(2),
                  pl.BlockSpec((None, 3, d, tn), lambda j, i: (layer, 0, 0, j))],
        out_specs=pl.BlockSpec((tm, tn), lambda j, i: (i, j)),
        out_shape=jax.ShapeDtypeStruct((m, d), BF16),
        compiler_params=_params(2, 56), name="branch_merge",
    )(u_ret, u_ssd, u_lru, pb, pb, pb, w_branch)


def _ffn_kernel(*refs, gated):
    if gated:
        x_ref, w1_ref, w3_ref, w2_ref, gate_ref, o_ref, acc = refs
    else:
        x_ref, w1_ref, w3_ref, w2_ref, o_ref, acc = refs
    f = pl.program_id(1)

    @pl.when(f == 0)
    def _():
        acc[...] = jnp.zeros_like(acc)

    x = x_ref[...]
    hid = _silu(_dot(x, w1_ref[...].astype(BF16))) * _dot(x, w3_ref[...].astype(BF16))
    if gated:
        hid = hid * gate_ref[...]
    acc[...] += _dot(hid.astype(BF16), w2_ref[...].astype(BF16))

    @pl.when(f == pl.num_programs(1) - 1)
    def _():
        o_ref[...] = acc[...].astype(o_ref.dtype)


def ffn_dense(h, w1, w3, w2, j, *, tm, tf):
    m, d = h.shape
    nf = w1.shape[-1] // tf
    return pl.pallas_call(
        functools.partial(_ffn_kernel, gated=False),
        grid=(m // tm, nf),
        in_specs=[pl.BlockSpec((tm, d), lambda i, f: (i, 0)),
                  pl.BlockSpec((None, d, tf), lambda i, f: (j, 0, f)),
                  pl.BlockSpec((None, d, tf), lambda i, f: (j, 0, f)),
                  pl.BlockSpec((None, tf, d), lambda i, f: (j, f, 0))],
        out_specs=pl.BlockSpec((tm, d), lambda i, f: (i, 0)),
        out_shape=jax.ShapeDtypeStruct((m, d), BF16),
        scratch_shapes=[pltpu.VMEM((tm, d), F32)],
        compiler_params=_params(2, 56), name="ffn_dense",
    )(h, w1, w3, w2)


def _moe_kernel(te_ref, nv_ref, src_ref, dst_ref, wgt_ref, x_hbm, w1_ref, w3_ref, w2_ref, y_hbm,
                xg, xb, acc, sem_in, sem_out, *, tm):
    i = pl.program_id(0)
    f = pl.program_id(1)
    nf = pl.num_programs(1)
    valid = i < nv_ref[0]

    def row_in(r):
        return pltpu.make_async_copy(x_hbm.at[pl.ds(src_ref[0, r], 1), :], xg.at[pl.ds(r, 1), :], sem_in)

    def row_out(r):
        return pltpu.make_async_copy(xg.at[pl.ds(r, 1), :], y_hbm.at[pl.ds(dst_ref[0, r], 1), :], sem_out)

    def for_rows(fn):
        def body(r, carry):
            fn(r)
            return carry
        lax.fori_loop(0, tm, body, 0, unroll=8)

    @pl.when(jnp.logical_and(valid, f == 0))
    def _():
        for_rows(lambda r: row_in(r).start())
        pltpu.make_async_copy(x_hbm.at[pl.ds(0, tm), :], xg, sem_in).wait()
        xb[...] = xg[...].astype(BF16)
        acc[...] = jnp.zeros_like(acc)

    @pl.when(valid)
    def _():
        x = xb[...]
        hid = _silu(_dot(x, w1_ref[...].astype(BF16))) * _dot(x, w3_ref[...].astype(BF16))
        acc[...] += _dot(hid.astype(BF16), w2_ref[...].astype(BF16))

    @pl.when(f == nf - 1)
    def _():
        xg[...] = jnp.where(valid, acc[...] * wgt_ref[...], 0.0)
        for_rows(lambda r: row_out(r).start())
        pltpu.make_async_copy(xg, y_hbm.at[pl.ds(0, tm), :], sem_out).wait()


def ffn_experts(h, route, w1, w3, w2, j, *, tm, tf):
    m, d = h.shape
    n_exp = w1.shape[1]
    nf = w1.shape[-1] // tf
    n_pairs = 2 * m
    n_tiles = n_pairs // tm + n_exp
    n_slots = n_tiles * tm

    e_flat = route[:, :2].astype(jnp.int32).T.reshape(-1)
    w_flat = route[:, 2:4].T.reshape(-1)
    order = jnp.argsort(e_flat, stable=True).astype(jnp.int32)
    counts = jnp.sum(e_flat[None, :] == jnp.arange(n_exp, dtype=jnp.int32)[:, None], axis=1).astype(jnp.int32)
    starts = jnp.cumsum(counts) - counts
    padded = ((counts + tm - 1) // tm) * tm
    pends = jnp.cumsum(padded)
    pstarts = pends - padded
    n_valid = (pends[-1] // tm).astype(jnp.int32)
    tile_start = jnp.arange(n_tiles, dtype=jnp.int32) * tm
    tile_e_raw = jnp.sum(tile_start[:, None] >= pends[None, :], axis=1).astype(jnp.int32)
    tile_e = jnp.minimum(tile_e_raw, tile_e_raw[jnp.maximum(n_valid - 1, 0)])
    slot_e = jnp.repeat(tile_e_raw, tm)
    slot_ec = jnp.minimum(slot_e, n_exp - 1)
    rank = jnp.arange(n_slots, dtype=jnp.int32) - pstarts[slot_ec]
    real = jnp.logical_and(slot_e < n_exp, rank < counts[slot_ec])
    pair = order[jnp.clip(starts[slot_ec] + rank, 0, n_pairs - 1)]
    src = jnp.where(real, pair % m, 0)
    slot_id = jnp.arange(n_slots, dtype=jnp.int32)
    pad_row = n_pairs + slot_id - (starts + counts)[slot_ec]
    dst = jnp.where(real, pair, jnp.where(slot_e < n_exp, pad_row, slot_id))
    wgt = jnp.where(real, w_flat[pair], 0.0)

    def w_idx(i, f, te, nv):
        return jnp.where(i < nv[0], f, nf - 1)

    grid_spec = pltpu.PrefetchScalarGridSpec(
        num_scalar_prefetch=2,
        grid=(n_tiles, nf),
        in_specs=[pl.BlockSpec((None, 1, tm), lambda i, f, te, nv: (i, 0, 0), memory_space=pltpu.SMEM),
                  pl.BlockSpec((None, 1, tm), lambda i, f, te, nv: (i, 0, 0), memory_space=pltpu.SMEM),
                  pl.BlockSpec((tm, 1), lambda i, f, te, nv: (i, 0)),
                  pl.BlockSpec(memory_space=pl.ANY),
                  pl.BlockSpec((None, None, d, tf), lambda i, f, te, nv: (j, te[i], 0, w_idx(i, f, te, nv))),
                  pl.BlockSpec((None, None, d, tf), lambda i, f, te, nv: (j, te[i], 0, w_idx(i, f, te, nv))),
                  pl.BlockSpec((None, None, tf, d), lambda i, f, te, nv: (j, te[i], w_idx(i, f, te, nv), 0))],
        out_specs=pl.BlockSpec(memory_space=pl.ANY),
        scratch_shapes=[pltpu.VMEM((tm, d), F32), pltpu.VMEM((tm, d), BF16), pltpu.VMEM((tm, d), F32),
                        pltpu.SemaphoreType.DMA(()), pltpu.SemaphoreType.DMA(())],
    )
    return pl.pallas_call(
        functools.partial(_moe_kernel, tm=tm),
        grid_spec=grid_spec,
        out_shape=jax.ShapeDtypeStruct((n_slots, d), F32),
        compiler_params=_params(2, 56), name="ffn_experts",
    )(tile_e, n_valid.reshape(1), src.reshape(n_tiles, 1, tm), dst.reshape(n_tiles, 1, tm),
      wgt.reshape(n_slots, 1), h, w1, w3, w2)


def _router_kernel(h_ref, w_ref, b_ref, o_ref):
    logits = _dot(h_ref[...].astype(BF16), w_ref[...].astype(BF16)) + b_ref[...]
    lane = lax.broadcasted_iota(jnp.int32, logits.shape, 1).astype(F32)
    neg = -jnp.inf
    l1 = jnp.where(lane < N_EXPERTS, logits, neg)
    m1 = jnp.max(l1, axis=-1, keepdims=True)
    i1 = jnp.min(jnp.where(l1 == m1, lane, 1e9), axis=-1, keepdims=True)
    l2 = jnp.where(lane == i1, neg, l1)
    m2 = jnp.max(l2, axis=-1, keepdims=True)
    i2 = jnp.min(jnp.where(l2 == m2, lane, 1e9), axis=-1, keepdims=True)
    e2 = jnp.exp(m2 - m1)
    den = 1.0 + e2
    o_ref[...] = (jnp.where(lane == 0.0, i1, 0.0) + jnp.where(lane == 1.0, i2, 0.0)
                  + jnp.where(lane == 2.0, 1.0 / den, 0.0) + jnp.where(lane == 3.0, e2 / den, 0.0))


def router(h, router_w, router_b, j, *, tm):
    m, d = h.shape
    return pl.pallas_call(
        _router_kernel,
        grid=(m // tm,),
        in_specs=[pl.BlockSpec((tm, d), lambda i: (i, 0)),
                  pl.BlockSpec((None, d, 128), lambda i: (j, 0, 0)),
                  pl.BlockSpec((None, 1, 128), lambda i: (j, 0, 0))],
        out_specs=pl.BlockSpec((tm, 128), lambda i: (i, 0)),
        out_shape=jax.ShapeDtypeStruct((m, 128), F32),
        compiler_params=_params(1, 32), name="router",
    )(h, router_w, router_b)


def kernel(x, c, ctx, c_ctx, w_mod, b_mod, norm1_g, norm2_g, w_in, ret_decay, ret_gn_g, ssd_conv_w, ssd_conv_b, ssd_dt_bias, ssd_a_log, ssd_d, ssd_norm_g, lru_conv_w, lru_conv_b, lru_gate_w, lru_gate_b, lru_lambda, w_branch, w_out, ffn_w1, ffn_w3, ffn_w2, router_w, router_b, moe_w1, moe_w3, moe_w2, final_g):
    nb, n, d = x.shape
    c_len = ctx.shape[1]
    depth = w_mod.shape[0]
    n_lat, n_ctx = nb * n, nb * c_len
    lat_tiles_per_seq = n // 512

    qkvgz_xbc = 11264
    tail0 = qkvgz_xbc + 2 * SSD_HEADS
    n_tail = w_in.shape[-1] - tail0
    w_in_t = jnp.swapaxes(w_in, 1, 2)
    w_br = w_branch.astype(BF16)

    c8 = jnp.concatenate([c, c_ctx[None, :], jnp.zeros((8 - nb - 1, d), F32)], axis=0)
    mods3 = modulation(c8, w_mod, b_mod).reshape(depth * 8 * 6, 1, d)
    seg_lat = lambda i: i // lat_tiles_per_seq
    seg_ctx = lambda i: nb
    log_g = jax.nn.log_sigmoid(ret_decay.astype(F32))
    dtb = jnp.pad(ssd_dt_bias.reshape(depth, 1, 2 * SSD_HEADS), ((0, 0), (0, 0), (0, 128 - 2 * SSD_HEADS)))
    aneg = jnp.pad(-jnp.exp(ssd_a_log.astype(F32)).reshape(depth, 1, 2 * SSD_HEADS),
                   ((0, 0), (0, 0), (0, 128 - 2 * SSD_HEADS)))
    dskip = jnp.repeat(ssd_d, SSD_HEADDIM, axis=-1).reshape(depth, 1, SSD_HEADS * SSD_HEADDIM)
    wcat = jnp.transpose(lru_gate_w, (0, 3, 4, 1, 2, 5)).reshape(depth, 16, LRU_BW, 4 * LRU_BW).astype(BF16)
    gate_b = lru_gate_b.reshape(depth, 4, -1)
    router_wp = jnp.pad(router_w, ((0, 0), (0, 0), (0, 128 - N_EXPERTS)))
    router_bp = jnp.pad(router_b, ((0, 0), (0, 128 - N_EXPERTS)))[:, None, :]

    rows = n // GRID_W
    row = jnp.repeat(jnp.arange(rows, dtype=F32), GRID_W)
    colp = jnp.tile(jnp.arange(GRID_W, dtype=F32), rows)
    n_freq = RET_DK // 4
    inv = ROPE_BASE ** (-jnp.arange(n_freq, dtype=F32) / n_freq)
    ang = jnp.concatenate([row[:, None] * inv, colp[:, None] * inv], axis=-1)
    cos2 = jnp.concatenate([jnp.cos(ang), jnp.cos(ang)], axis=-1)
    sin2 = jnp.concatenate([-jnp.sin(ang), jnp.sin(ang)], axis=-1)

    xl = x.reshape(n_lat, d)
    xc = ctx.reshape(n_ctx, d)
    yl = yc = None
    ctx_rb = n_lat // c_len

    for i in range(depth):
        last = i == depth - 1
        prev_gate = (i - 1, 5)
        g1 = norm1_g[i][None, :]
        xl, hl = resnorm(xl, yl, mods3, g1, seg_fn=seg_lat, gate=prev_gate, sc=(i, 1), sh=(i, 0), out_dtype=BF16, tm=512)
        xc, hc = resnorm(xc, yc, mods3, g1, seg_fn=seg_ctx, gate=prev_gate, sc=(i, 1), sh=(i, 0), out_dtype=BF16, tm=512)
        h = jnp.concatenate([hl, hc], axis=0)
        tm_all = h.shape[0] // 8

        pa = matmul_nt(h, w_in_t, i, row0=0, n_cols=qkvgz_xbc, tn=1024, tm=tm_all, out_dtype=BF16)
        pb = matmul_nt(h, w_in_t, i, row0=tail0, n_cols=n_tail, tn=1024, tm=tm_all, out_dtype=BF16)
        pdt = matmul_nt(h, w_in_t, i, row0=qkvgz_xbc, n_cols=128, tn=128, tm=tm_all, out_dtype=F32)

        xbc_l = dwconv(pa, ssd_conv_w, ssd_conv_b, i, n=n, nb=nb, row_block0=0, col0=8192, n_cols=3072, act=True, out_dtype=BF16)
        xbc_c = dwconv(pa, ssd_conv_w, ssd_conv_b, i, n=c_len, nb=nb, row_block0=ctx_rb, col0=8192, n_cols=3072, act=True, out_dtype=BF16)
        xr_l = dwconv(pb, lru_conv_w, lru_conv_b, i, n=n, nb=nb, row_block0=0, col0=2048, n_cols=2048, act=False, out_dtype=F32)
        xr_c = dwconv(pb, lru_conv_w, lru_conv_b, i, n=c_len, nb=nb, row_block0=ctx_rb, col0=2048, n_cols=2048, act=False, out_dtype=F32)

        ret_l, ret_c = retention(pa, cos2, sin2, log_g[i], ret_gn_g, i, nb=nb, n=n, c=c_len)
        ssd_l, ssd_c = ssd(xbc_l, xbc_c, pdt, pa, dtb, aneg, dskip, i, nb=nb, n=n, c=c_len)
        lru_l, lru_c = rglru(xr_l, xr_c, pb, wcat, gate_b, lru_lambda, i, nb=nb, n=n, c=c_len)
        sg = ssd_norm_g[i][None, :]

        dense = i % 2 == 0
        h2_dtype = BF16 if dense else F32

        def mixer_out(xres, u_ret, y_ssd, u_lru, seg_fn, row_block0, tm):
            _, u_ssd = resnorm(y_ssd, None, mods3, sg, seg_fn=seg_fn, gate=None, sc=None, sh=None, out_dtype=BF16, tm=512)
            merged = branch_merge(u_ret, u_ssd, u_lru, pb, w_br, i, row_block0=row_block0, merge_col0=4096, tm=tm, tn=512)
            y = matmul(merged, w_out, (i,), col_block0=0, n_cols=d, tn=1024, tm=tm, out_dtype=BF16)
            return resnorm(xres, y, mods3, norm2_g[i][None, :], seg_fn=seg_fn, gate=(i, 2), sc=(i, 4), sh=(i, 3),
                           out_dtype=h2_dtype, tm=512)

        xl, h2l = mixer_out(xl, ret_l, ssd_l, lru_l, seg_lat, 0, 1024)
        if not last:
            xc, h2c = mixer_out(xc, ret_c, ssd_c, lru_c, seg_ctx, n_lat // 512, 512)
        j = i // 2
        if dense:
            yl = ffn_dense(h2l, ffn_w1, ffn_w3, ffn_w2, j, tm=1024, tf=512)
            if not last:
                yc = ffn_dense(h2c, ffn_w1, ffn_w3, ffn_w2, j, tm=512, tf=512)
        else:
            h2 = h2l if last else jnp.concatenate([h2l, h2c], axis=0)
            route = router(h2, router_wp, router_bp, j, tm=512)
            y2 = ffn_experts(h2, route, moe_w1, moe_w3, moe_w2, j, tm=512, tf=512)
            mb = h2.shape[0] // 512
            yl = [(y2, 0), (y2, mb)]
            yc = [(y2, n_lat // 512), (y2, mb + n_lat // 512)]

    _, out = resnorm(xl, yl, mods3, final_g[None, :], seg_fn=seg_lat, gate=(depth - 1, 5), sc=None, sh=None, out_dtype=F32, tm=512)
    return out.reshape(nb, n, d)
```

```python
import functools

import jax
import jax.numpy as jnp
from jax import lax
from jax.experimental import pallas as pl
from jax.experimental.pallas import tpu as pltpu

F32 = jnp.float32
BF16 = jnp.bfloat16

EPS = 1e-6
CHUNK = 128
CONV_W = 4
RET_HEADS = 8
RET_DK = 128
RET_DV = 256
SSD_HEADS = 32
SSD_HEADDIM = 64
SSD_GROUPS = 4
SSD_STATE = 128
LRU_BW = 128
LRU_C = 8.0
N_EXPERTS = 8
ROPE_BASE = 10000.0
GRID_W = 64

MIB = 1024 * 1024

_NT = (((1,), (1,)), ((), ()))
_TN = (((0,), (0,)), ((), ()))


def _params(n_axes, vmem_mib):
    return pltpu.CompilerParams(dimension_semantics=("arbitrary",) * n_axes,
                                vmem_limit_bytes=int(vmem_mib * MIB))


def _silu(x):
    return x * jax.nn.sigmoid(x)


def _dot(a, b):
    return jnp.dot(a, b, preferred_element_type=F32)


def _mod_kernel(c_ref, w_ref, b_ref, o_ref):
    x = _silu(c_ref[...]).astype(BF16)
    o_ref[...] = _dot(x, w_ref[...].astype(BF16)) + b_ref[...]


def modulation(c8, w_mod, b_mod):
    depth, d, n6 = w_mod.shape
    tn = 1536
    return pl.pallas_call(
        _mod_kernel,
        grid=(depth, n6 // tn),
        in_specs=[pl.BlockSpec((8, d), lambda l, j: (0, 0)),
                  pl.BlockSpec((None, d, tn), lambda l, j: (l, 0, j)),
                  pl.BlockSpec((None, 1, tn), lambda l, j: (l, 0, j))],
        out_specs=pl.BlockSpec((None, 8, tn), lambda l, j: (l, 0, j)),
        out_shape=jax.ShapeDtypeStruct((depth, 8, n6), F32),
        compiler_params=_params(2, 48),
        name="modulation",
    )(c8, w_mod, b_mod.reshape(depth, 1, n6))


def _resnorm_kernel(*refs, n_y, has_mod):
    refs = list(refs)
    x_ref = refs.pop(0)
    has_y = n_y > 0
    if has_y:
        y_refs = [refs.pop(0) for _ in range(n_y)]
        gate_ref = refs.pop(0)
    g_ref = refs.pop(0)
    if has_mod:
        sc_ref, sh_ref = refs.pop(0), refs.pop(0)
    x = x_ref[...].astype(F32)
    if has_y:
        xo_ref = refs.pop(0)
        y = y_refs[0][...].astype(F32)
        for y_ref in y_refs[1:]:
            y = y + y_ref[...].astype(F32)
        x = x + gate_ref[...] * y
        xo_ref[...] = x
    h_ref = refs.pop(0)
    ms = jnp.mean(x * x, axis=-1, keepdims=True)
    yn = x * lax.rsqrt(ms + EPS) * g_ref[...]
    if has_mod:
        yn = yn * (1.0 + sc_ref[...]) + sh_ref[...]
    h_ref[...] = yn.astype(h_ref.dtype)


def resnorm(x, y, mods3, norm_g, *, seg_fn, gate, sc, sh, out_dtype, tm):
    m, d = x.shape
    if y is None:
        ys = []
    elif isinstance(y, list):
        ys = y
    else:
        ys = [(y, 0)]
    has_y = bool(ys)
    has_mod = sc is not None

    def mod_spec(layer_chunk):
        layer, chunk = layer_chunk
        return pl.BlockSpec((None, 1, d), lambda i: ((layer * 8 + seg_fn(i)) * 6 + chunk, 0, 0))

    row = pl.BlockSpec((tm, d), lambda i: (i, 0))
    in_specs, args = [row], [x]
    if has_y:
        for arr, off in ys:
            in_specs.append(pl.BlockSpec((tm, d), lambda i, off=off: (off + i, 0)))
            args.append(arr)
        in_specs.append(mod_spec(gate))
        args.append(mods3)
    in_specs.append(pl.BlockSpec((1, d), lambda i: (0, 0)))
    args.append(norm_g)
    if has_mod:
        in_specs += [mod_spec(sc), mod_spec(sh)]
        args += [mods3, mods3]
    out_specs, out_shape = [], []
    if has_y:
        out_specs.append(row)
        out_shape.append(jax.ShapeDtypeStruct((m, d), F32))
    out_specs.append(row)
    out_shape.append(jax.ShapeDtypeStruct((m, d), out_dtype))
    res = pl.pallas_call(
        functools.partial(_resnorm_kernel, n_y=len(ys), has_mod=has_mod),
        grid=(m // tm,), in_specs=in_specs, out_specs=out_specs, out_shape=out_shape,
        compiler_params=_params(1, 48), name="resnorm",
    )(*args)
    return (res[0], res[1]) if has_y else (x, res[0])


def _mm_kernel(x_ref, w_ref, o_ref, *scratch):
    if scratch:
        wb = scratch[0]

        @pl.when(pl.program_id(1) == 0)
        def _():
            wb[...] = w_ref[...].astype(BF16)

        w = wb[...]
    else:
        w = w_ref[...]
    o_ref[...] = _dot(x_ref[...], w).astype(o_ref.dtype)


def matmul(x, w, prefix, *, col_block0, n_cols, tn, tm, out_dtype, m_rows=None):
    m, k = x.shape
    m = m_rows or m
    cast = w.dtype != BF16
    lead = (None,) * len(prefix)
    return pl.pallas_call(
        _mm_kernel,
        grid=(n_cols // tn, m // tm),
        in_specs=[pl.BlockSpec((tm, k), lambda j, i: (i, 0)),
                  pl.BlockSpec(lead + (k, tn), lambda j, i: tuple(prefix) + (0, col_block0 + j))],
        out_specs=pl.BlockSpec((tm, tn), lambda j, i: (i, j)),
        out_shape=jax.ShapeDtypeStruct((m, n_cols), out_dtype),
        scratch_shapes=[pltpu.VMEM((k, tn), BF16)] if cast else [],
        compiler_params=_params(2, 56), name="matmul",
    )(x, w)


def _mm_nt_kernel(x_ref, w_ref, o_ref, wb):
    @pl.when(pl.program_id(1) == 0)
    def _():
        wb[...] = w_ref[...].astype(BF16)

    o_ref[...] = lax.dot_general(x_ref[...], wb[...], _NT, preferred_element_type=F32).astype(o_ref.dtype)


def matmul_nt(x, w_t, layer, *, row0, n_cols, tn, tm, out_dtype):
    m, k = x.shape
    return pl.pallas_call(
        _mm_nt_kernel,
        grid=(n_cols // tn, m // tm),
        in_specs=[pl.BlockSpec((tm, k), lambda j, i: (i, 0)),
                  pl.BlockSpec((None, pl.Element(tn), pl.Element(k)),
                               lambda j, i: (layer, pl.multiple_of(row0 + j * tn, 8), 0))],
        out_specs=pl.BlockSpec((tm, tn), lambda j, i: (i, j)),
        out_shape=jax.ShapeDtypeStruct((m, n_cols), out_dtype),
        scratch_shapes=[pltpu.VMEM((tn, k), BF16)],
        compiler_params=_params(2, 56), name="matmul_nt",
    )(x, w_t)


def _conv_kernel(x_ref, w_ref, b_ref, o_ref, scr, *, n, act, rc):
    tc = x_ref.shape[1]
    scr[0:8, :] = jnp.zeros((8, tc), F32)
    scr[8 + n:16 + n, :] = jnp.zeros((8, tc), F32)

    def copy_in(c, carry):
        r0 = pl.multiple_of(c * rc, rc)
        scr[pl.ds(8 + r0, rc), :] = x_ref[pl.ds(r0, rc), :].astype(F32)
        return carry

    lax.fori_loop(0, n // rc, copy_in, 0)
    w = w_ref[...]
    b = b_ref[...]

    def conv(c, carry):
        r0 = pl.multiple_of(c * rc, rc)
        win = scr[pl.ds(r0, rc + 16), :]
        acc = b + w[2:3] * win[8:8 + rc]
        for j in (0, 1, 3):
            rolled = pltpu.roll(win, (2 - j) % (rc + 16), 0)
            acc = acc + w[j:j + 1] * rolled[8:8 + rc]
        if act:
            acc = _silu(acc)
        o_ref[pl.ds(r0, rc), :] = acc.astype(o_ref.dtype)
        return carry

    lax.fori_loop(0, n // rc, conv, 0)


def dwconv(p, conv_w, conv_b, layer, *, n, nb, row_block0, col0, n_cols, act, out_dtype):
    tc = 512
    rc = 64
    depth = conv_w.shape[0]
    return pl.pallas_call(
        functools.partial(_conv_kernel, n=n, act=act, rc=rc),
        grid=(nb, n_cols // tc),
        in_specs=[pl.BlockSpec((n, tc), lambda b, j: (row_block0 + b, col0 // tc + j)),
                  pl.BlockSpec((None, CONV_W, tc), lambda b, j: (layer, 0, j)),
                  pl.BlockSpec((None, 1, tc), lambda b, j: (layer, 0, j))],
        out_specs=pl.BlockSpec((n, tc), lambda b, j: (b, j)),
        out_shape=jax.ShapeDtypeStruct((nb * n, n_cols), out_dtype),
        scratch_shapes=[pltpu.VMEM((n + 16, tc), F32)],
        compiler_params=_params(2, 48), name="dwconv",
    )(p, conv_w, conv_b.reshape(depth, 1, -1))


def _ret_kernel(lg_ref, ql, kl, vl, gl, qc, kc, vc, gc, cos_ref, sin_ref, gn_ref,
                ul_ref, uc_ref, qs, ks, acc, *, n, c):
    h = pl.program_id(1)
    lgf = lg_ref[0, h]
    lgb = lg_ref[1, h]
    ii = lax.broadcasted_iota(jnp.int32, (CHUNK, CHUNK), 0).astype(F32)
    jj = lax.broadcasted_iota(jnp.int32, (CHUNK, CHUNK), 1).astype(F32)
    diff = ii - jj
    dsum = (jnp.where(diff >= 0, jnp.exp(lgf * jnp.maximum(diff, 0.0)), 0.0)
            + jnp.where(diff <= 0, jnp.exp(lgb * jnp.maximum(-diff, 0.0)), 0.0))
    col = lax.broadcasted_iota(jnp.int32, (CHUNK, 1), 0).astype(F32)
    kdec_f = jnp.exp(lgf * (CHUNK - 1.0 - col))
    kdec_b = jnp.exp(lgb * col)
    qdec_f = jnp.exp(lgf * (col + 1.0))
    qdec_b = jnp.exp(lgb * (CHUNK - col))
    one = jnp.ones((1, 1), F32)
    cd_f = jnp.exp(one * (lgf * CHUNK))
    cd_b = jnp.exp(one * (lgb * CHUNK))
    scale = RET_DK ** -0.5

    def fwd_chunk(r0, q, k, v, s_in):
        qs[pl.ds(r0, CHUNK), :] = q
        ks[pl.ds(r0, CHUNK), :] = k
        s = lax.dot_general(q, k, _NT, preferred_element_type=F32)
        y = _dot((s * dsum).astype(BF16), v)
        y = y + _dot((q.astype(F32) * qdec_f).astype(BF16), s_in.astype(BF16))
        kv = lax.dot_general((k.astype(F32) * kdec_f).astype(BF16), v, _TN, preferred_element_type=F32)
        acc[pl.ds(r0, CHUNK), :] = y
        return cd_f * s_in + kv

    def bwd_chunk(r0, v, s_in):
        q = qs[pl.ds(r0, CHUNK), :]
        k = ks[pl.ds(r0, CHUNK), :]
        y = _dot((q.astype(F32) * qdec_b).astype(BF16), s_in.astype(BF16))
        acc[pl.ds(r0, CHUNK), :] = acc[pl.ds(r0, CHUNK), :] + y
        kv = lax.dot_general((k.astype(F32) * kdec_b).astype(BF16), v, _TN, preferred_element_type=F32)
        return cd_b * s_in + kv

    def rot(t_ref, r):
        t = t_ref[pl.ds(r, CHUNK), :].astype(F32)
        return t * cos_ref[pl.ds(r, CHUNK), :] + pltpu.roll(t, RET_DK // 2, 1) * sin_ref[pl.ds(r, CHUNK), :]

    s_f = jnp.zeros((RET_DK, RET_DV), F32)
    for ci in range(c // CHUNK):
        r = ci * CHUNK
        q = qc[r:r + CHUNK, :]
        k = (kc[r:r + CHUNK, :].astype(F32) * scale).astype(BF16)
        s_f = fwd_chunk(r, q, k, vc[r:r + CHUNK, :], s_f)

    def fwd_lat(i, s_in):
        r = pl.multiple_of(i * CHUNK, CHUNK)
        q = rot(ql, r).astype(BF16)
        k = (rot(kl, r) * scale).astype(BF16)
        return fwd_chunk(c + r, q, k, vl[pl.ds(r, CHUNK), :], s_in)

    lax.fori_loop(0, n // CHUNK, fwd_lat, s_f, unroll=4)

    s_b = jnp.zeros((RET_DK, RET_DV), F32)
    for ci in reversed(range(c // CHUNK)):
        r = ci * CHUNK
        s_b = bwd_chunk(r, vc[r:r + CHUNK, :], s_b)

    def bwd_lat(i, s_in):
        r = pl.multiple_of(n - CHUNK - i * CHUNK, CHUNK)
        return bwd_chunk(c + r, vl[pl.ds(r, CHUNK), :], s_in)

    lax.fori_loop(0, n // CHUNK, bwd_lat, s_b, unroll=4)

    gn = gn_ref[...]

    def finish(y, g):
        mu = jnp.mean(y, axis=-1, keepdims=True)
        yc = y - mu
        var = jnp.mean(yc * yc, axis=-1, keepdims=True)
        return (_silu(g.astype(F32)) * (yc * lax.rsqrt(var + EPS) * gn)).astype(BF16)

    for ci in range(c // CHUNK):
        r = ci * CHUNK
        uc_ref[r:r + CHUNK, :] = finish(acc[r:r + CHUNK, :], gc[r:r + CHUNK, :])

    def fin_lat(i, carry):
        r = pl.multiple_of(i * CHUNK, CHUNK)
        ul_ref[pl.ds(r, CHUNK), :] = finish(acc[pl.ds(c + r, CHUNK), :], gl[pl.ds(r, CHUNK), :])
        return carry

    lax.fori_loop(0, n // CHUNK, fin_lat, 0, unroll=4)


def retention(pa, cos2, sin2, log_g, gn_g, layer, *, nb, n, c):
    rb = (nb * n) // c
    lat = lambda w, off: pl.BlockSpec((n, w), lambda b, h: (b, off + h))
    ctx = lambda w, off: pl.BlockSpec((c, w), lambda b, h: (rb + b, off + h))
    full = lambda: pl.BlockSpec((n, RET_DK), lambda b, h: (0, 0))
    depth = gn_g.shape[0]
    return pl.pallas_call(
        functools.partial(_ret_kernel, n=n, c=c),
        grid=(nb, RET_HEADS),
        in_specs=[pl.BlockSpec(memory_space=pltpu.SMEM),
                  lat(RET_DK, 0), lat(RET_DK, 8), lat(RET_DV, 8), lat(RET_DV, 16),
                  ctx(RET_DK, 0), ctx(RET_DK, 8), ctx(RET_DV, 8), ctx(RET_DV, 16),
                  full(), full(),
                  pl.BlockSpec((None, 1, RET_DV), lambda b, h: (layer, 0, h))],
        out_specs=[pl.BlockSpec((n, RET_DV), lambda b, h: (b, h)),
                   pl.BlockSpec((c, RET_DV), lambda b, h: (b, h))],
        out_shape=[jax.ShapeDtypeStruct((nb * n, RET_HEADS * RET_DV), BF16),
                   jax.ShapeDtypeStruct((nb * c, RET_HEADS * RET_DV), BF16)],
        scratch_shapes=[pltpu.VMEM((c + n, RET_DK), BF16), pltpu.VMEM((c + n, RET_DK), BF16),
                        pltpu.VMEM((c + n, RET_DV), F32)],
        compiler_params=_params(2, 48), name="retention",
    )(log_g, pa, pa, pa, pa, pa, pa, pa, pa, cos2, sin2, gn_g.reshape(depth, 1, -1))


def _split3(x):
    hi = x.astype(BF16)
    r = x - hi.astype(F32)
    mid = r.astype(BF16)
    lo = (r - mid.astype(F32)).astype(BF16)
    return hi, mid, lo


def _dot_split_l(x, m, terms):
    parts = _split3(x)[:terms]
    out = _dot(parts[0], m)
    for p in parts[1:]:
        out = out + _dot(p, m)
    return out


def _dot_split_r(m, x, terms):
    parts = _split3(x)[:terms]
    out = _dot(m, parts[0])
    for p in parts[1:]:
        out = out + _dot(m, p)
    return out


def _ssd_kernel(xl, bl, cl, dtl, zl, xc, bc, cc, dtc, zc, dtb_ref, an_ref, dsk_ref,
                yl_ref, yc_ref, acc, s_f, s_b, pt_f, pt_b, xw_b, ex_b, tt_b, *, n, c):
    g = pl.program_id(1)
    L = CHUNK
    hg = SSD_HEADS // SSD_GROUPS
    gw = hg * SSD_HEADDIM
    ii = lax.broadcasted_iota(jnp.int32, (L, L), 0)
    jj = lax.broadcasted_iota(jnp.int32, (L, L), 1)
    causal = ii >= jj
    anti = ii <= jj
    tril = causal.astype(BF16)
    triu = anti.astype(BF16)
    kk = lax.broadcasted_iota(jnp.int32, (L, gw), 0)
    mm = lax.shift_right_logical(lax.broadcasted_iota(jnp.int32, (L, gw), 1), 6)
    e_f = (kk == hg * g + mm).astype(BF16)
    e_b = (kk == SSD_HEADS + hg * g + mm).astype(BF16)
    lane2 = lax.broadcasted_iota(jnp.int32, (L, 2 * SSD_HEADDIM), 1)
    left = lane2 < SSD_HEADDIM
    dtb = dtb_ref[...]
    an = an_ref[...]
    dsk = dsk_ref[...]

    def fwd_chunk(r0, x, bm, cm, dt_raw):
        xf = x.astype(F32)
        dt = jax.nn.softplus(dt_raw + dtb)
        a = dt * an
        psm_f = _dot_split_r(tril, a, 2)
        psm_b = psm_f[L - 1:L, :] - psm_f + a
        pexp_f = _dot_split_l(psm_f, e_f, 2)
        pexp_b = _dot_split_l(psm_b, e_b, 2)
        dtb16 = dt.astype(BF16)
        pt_f[...] = psm_f.T
        pt_b[...] = psm_b.T
        xdt_f = xf * _dot(dtb16, e_f)
        xdt_b = xf * _dot(dtb16, e_b)
        tot_b = pexp_b[0:1, :]
        xw_b[pl.ds(r0, L), :] = (xdt_b * jnp.exp(tot_b - pexp_b)).astype(BF16)
        ex_b[pl.ds(r0, L), :] = jnp.exp(pexp_b).astype(BF16)
        tt_b[pl.ds(r0 // L, 1)] = jnp.broadcast_to(jnp.exp(tot_b), (1, 8, gw))
        xcat = jnp.concatenate([xdt_f.astype(BF16), xdt_b.astype(BF16)], axis=0)
        cb = lax.dot_general(cm, bm, _NT, preferred_element_type=F32)
        for jp in range(hg // 2):
            lo = jp * 2 * SSD_HEADDIM
            xpair = xcat[:, lo:lo + 2 * SSD_HEADDIM]
            ys = []
            for j in (2 * jp, 2 * jp + 1):
                col = j * SSD_HEADDIM
                seg_f = pexp_f[:, col:col + 1] - pt_f[pl.ds(hg * g + j, 1), :]
                seg_b = pexp_b[:, col:col + 1] - pt_b[pl.ds(SSD_HEADS + hg * g + j, 1), :]
                m_f = cb * jnp.exp(jnp.where(causal, seg_f, -jnp.inf))
                m_b = cb * jnp.exp(jnp.where(anti, seg_b, -jnp.inf))
                mcat = jnp.concatenate([m_f.astype(BF16), m_b.astype(BF16)], axis=1)
                ys.append(_dot(mcat, xpair))
            acc[pl.ds(r0, L), lo:lo + 2 * SSD_HEADDIM] = jnp.where(left, ys[0], ys[1])
        s_in = s_f[...]
        tot = pexp_f[L - 1:L, :]
        y_off = _dot(cm, s_in.astype(BF16)) * jnp.exp(pexp_f)
        st = lax.dot_general(bm, (xdt_f * jnp.exp(tot - pexp_f)).astype(BF16), _TN, preferred_element_type=F32)
        s_f[...] = jnp.exp(tot) * s_in + st
        acc[pl.ds(r0, L), :] = acc[pl.ds(r0, L), :] + y_off + dsk * xf

    def bwd_chunk(r0, bm, cm):
        s_in = s_b[...]
        y_off = _dot(cm, s_in.astype(BF16)) * ex_b[pl.ds(r0, L), :].astype(F32)
        st = lax.dot_general(bm, xw_b[pl.ds(r0, L), :], _TN, preferred_element_type=F32)
        s_b[...] = tt_b[r0 // L][0:1, :] * s_in + st
        acc[pl.ds(r0, L), :] = acc[pl.ds(r0, L), :] + y_off

    s_f[...] = jnp.zeros_like(s_f)
    s_b[...] = jnp.zeros_like(s_b)
    nc_ctx = c // L
    for ci in range(nc_ctx):
        r = ci * L
        fwd_chunk(r, xc[r:r + L, :], bc[r:r + L, :], cc[r:r + L, :], dtc[r:r + L, :])

    def fwd_lat(i, carry):
        r = pl.multiple_of(i * L, L)
        fwd_chunk(c + r, xl[pl.ds(r, L), :], bl[pl.ds(r, L), :], cl[pl.ds(r, L), :], dtl[pl.ds(r, L), :])
        return carry

    lax.fori_loop(0, n // L, fwd_lat, 0, unroll=2)

    for ci in reversed(range(nc_ctx)):
        r = ci * L
        bwd_chunk(r, bc[r:r + L, :], cc[r:r + L, :])

    def bwd_lat(i, carry):
        r = pl.multiple_of(n - L - i * L, L)
        bwd_chunk(c + r, bl[pl.ds(r, L), :], cl[pl.ds(r, L), :])
        return carry

    lax.fori_loop(0, n // L, bwd_lat, 0, unroll=4)

    for ci in range(nc_ctx):
        r = ci * L
        yc_ref[r:r + L, :] = (acc[r:r + L, :] * _silu(zc[r:r + L, :].astype(F32))).astype(BF16)

    def fin_lat(i, carry):
        r = pl.multiple_of(i * L, L)
        yl_ref[pl.ds(r, L), :] = (acc[pl.ds(c + r, L), :] * _silu(zl[pl.ds(r, L), :].astype(F32))).astype(BF16)
        return carry

    lax.fori_loop(0, n // L, fin_lat, 0)


def ssd(xbc_l, xbc_c, pdt, pa, dtb, aneg, dskip, layer, *, nb, n, c):
    hg = SSD_HEADS // SSD_GROUPS
    gw = hg * SSD_HEADDIM
    rb = (nb * n) // c
    inner = SSD_HEADS * SSD_HEADDIM
    xoff = 0
    boff = inner // SSD_STATE
    coff = boff + SSD_GROUPS
    zoff = 6144 // gw
    lat = lambda w, off: pl.BlockSpec((n, w), lambda b, g: (b, off + g))
    ctx = lambda w, off: pl.BlockSpec((c, w), lambda b, g: (b, off + g))
    ctxp = lambda w, off: pl.BlockSpec((c, w), lambda b, g: (rb + b, off + g))
    vec = lambda w: pl.BlockSpec((None, 1, w), lambda b, g: (layer, 0, 0))
    return pl.pallas_call(
        functools.partial(_ssd_kernel, n=n, c=c),
        grid=(nb, SSD_GROUPS),
        in_specs=[lat(gw, xoff), lat(SSD_STATE, boff), lat(SSD_STATE, coff),
                  pl.BlockSpec((n, 128), lambda b, g: (b, 0)), lat(gw, zoff),
                  ctx(gw, xoff), ctx(SSD_STATE, boff), ctx(SSD_STATE, coff),
                  pl.BlockSpec((c, 128), lambda b, g: (rb + b, 0)), ctxp(gw, zoff),
                  vec(128), vec(128),
                  pl.BlockSpec((None, 1, gw), lambda b, g: (layer, 0, g))],
        out_specs=[pl.BlockSpec((n, gw), lambda b, g: (b, g)),
                   pl.BlockSpec((c, gw), lambda b, g: (b, g))],
        out_shape=[jax.ShapeDtypeStruct((nb * n, inner), BF16),
                   jax.ShapeDtypeStruct((nb * c, inner), BF16)],
        scratch_shapes=[pltpu.VMEM((c + n, gw), F32),
                        pltpu.VMEM((SSD_STATE, gw), F32), pltpu.VMEM((SSD_STATE, gw), F32),
                        pltpu.VMEM((128, CHUNK), F32), pltpu.VMEM((128, CHUNK), F32),
                        pltpu.VMEM((c + n, gw), BF16), pltpu.VMEM((c + n, gw), BF16),
                        pltpu.VMEM(((c + n) // CHUNK, 8, gw), F32)],
        compiler_params=_params(2, 60), name="ssd",
    )(xbc_l, xbc_l, xbc_l, pdt, pa, xbc_c, xbc_c, xbc_c, pdt, pa, dtb, aneg, dskip)


def _gelu_tanh(x):
    return 0.5 * x * (1.0 + jnp.tanh(0.7978845608028654 * (x + 0.044715 * (x * x * x))))


LRU_SEGS = 8


def _lru_kernel(xl, gl, xc, gc, w_ref, b_ref, lam_ref, ul_ref, uc_ref,
                ca_f, ch_f, ca_b, ch_b, p_f, h_f, p_b, h_b, cin, *, n, c, rc):
    seg = n // LRU_SEGS
    nsb = xl.shape[1] // LRU_BW
    sp = jax.nn.softplus(-lam_ref[...])
    bias = b_ref[...]

    def gates(x_ref, r0, store):
        x = x_ref[pl.ds(r0, rc), :]
        xb = x.astype(BF16)
        for sb in range(nsb):
            lanes = slice(sb * LRU_BW, (sb + 1) * LRU_BW)
            pre = _dot(xb[:, lanes], w_ref[sb])
            for d in range(2):
                r = jax.nn.sigmoid(pre[:, (2 * d) * LRU_BW:(2 * d + 1) * LRU_BW] + bias[2 * d:2 * d + 1, lanes])
                i = jax.nn.sigmoid(pre[:, (2 * d + 1) * LRU_BW:(2 * d + 2) * LRU_BW] + bias[2 * d + 1:2 * d + 2, lanes])
                log_a = -LRU_C * r * sp[d:d + 1, lanes]
                a = jnp.exp(log_a)
                mult = jnp.sqrt(jnp.maximum(1.0 - a * a, 0.0))
                store(d, sb, a, mult * (i * x[:, lanes]))

    def store_ctx(r0):
        def store(d, sb, a, bx):
            lanes = slice(sb * LRU_BW, (sb + 1) * LRU_BW)
            (ca_f, ca_b)[d][pl.ds(r0, rc), lanes] = a
            (ch_f, ch_b)[d][pl.ds(r0, rc), lanes] = bx
        return store

    for ci in range(c // rc):
        gates(xc, ci * rc, store_ctx(ci * rc))

    def ctx_step(i, carry):
        hf, hb = carry
        t_b = c - 1 - i
        hf = ca_f[pl.ds(i, 1), :] * hf + ch_f[pl.ds(i, 1), :]
        ch_f[pl.ds(i, 1), :] = hf
        hb = ca_b[pl.ds(t_b, 1), :] * hb + ch_b[pl.ds(t_b, 1), :]
        ch_b[pl.ds(t_b, 1), :] = hb
        return hf, hb

    zero = jnp.zeros((1, xl.shape[1]), F32)
    hf_ctx, hb_ctx = lax.fori_loop(0, c, ctx_step, (zero, zero), unroll=8)

    for ci in range(c // rc):
        r = ci * rc
        uc_ref[r:r + rc, :] = ((ch_f[r:r + rc, :] + ch_b[r:r + rc, :]) * _gelu_tanh(gc[r:r + rc, :].astype(F32))).astype(BF16)

    def gates_lat(ci, carry):
        r0 = pl.multiple_of(ci * rc, rc)
        s = r0 // seg
        row0 = (r0 - s * seg) * LRU_SEGS + s

        def store(d, sb, a, bx):
            (p_f, p_b)[d][sb, pl.ds(row0, rc, stride=LRU_SEGS), :] = a
            (h_f, h_b)[d][sb, pl.ds(row0, rc, stride=LRU_SEGS), :] = bx

        gates(xl, r0, store)
        return carry

    lax.fori_loop(0, n // rc, gates_lat, 0, unroll=2)

    def lat_step(i, carry):
        out = []
        for d, (p_s, h_s) in enumerate(((p_f, h_f), (p_b, h_b))):
            tt = i if d == 0 else seg - 1 - i
            row = pl.multiple_of(tt * LRU_SEGS, LRU_SEGS)
            for sb in range(nsb):
                h, p = carry[d * nsb + sb]
                a = p_s[sb, pl.ds(row, LRU_SEGS), :]
                h = a * h + h_s[sb, pl.ds(row, LRU_SEGS), :]
                p = a * p
                h_s[sb, pl.ds(row, LRU_SEGS), :] = h
                p_s[sb, pl.ds(row, LRU_SEGS), :] = p
                out.append((h, p))
        return tuple(out)

    tile0 = (jnp.zeros((LRU_SEGS, LRU_BW), F32), jnp.ones((LRU_SEGS, LRU_BW), F32))
    ends = lax.fori_loop(0, seg, lat_step, (tile0,) * (2 * nsb), unroll=4)

    sub = lax.broadcasted_iota(jnp.int32, (LRU_SEGS, LRU_BW), 0)
    for d in range(2):
        for sb in range(nsb):
            lanes = slice(sb * LRU_BW, (sb + 1) * LRU_BW)
            h_end, p_end = ends[d * nsb + sb]
            cur = jnp.broadcast_to((hf_ctx, hb_ctx)[d][:, lanes], (LRU_SEGS, LRU_BW))
            cmat = jnp.zeros((LRU_SEGS, LRU_BW), F32)
            order = range(LRU_SEGS) if d == 0 else reversed(range(LRU_SEGS))
            for s in order:
                cmat = jnp.where(sub == s, cur, cmat)
                nxt = h_end + p_end * cur
                cur = jnp.broadcast_to(nxt[s:s + 1, :], (LRU_SEGS, LRU_BW))
            cin[d * nsb + sb] = cmat

    def fin_lat(ci, carry):
        r0 = pl.multiple_of(ci * rc, rc)
        s = r0 // seg
        row0 = (r0 - s * seg) * LRU_SEGS + s
        g = _gelu_tanh(gl[pl.ds(r0, rc), :].astype(F32))
        for sb in range(nsb):
            lanes = slice(sb * LRU_BW, (sb + 1) * LRU_BW)
            y = None
            for d, (p_s, h_s) in enumerate(((p_f, h_f), (p_b, h_b))):
                rows = pl.ds(row0, rc, stride=LRU_SEGS)
                t = h_s[sb, rows, :] + p_s[sb, rows, :] * cin[d * nsb + sb, pl.ds(s, 1), :]
                y = t if y is None else y + t
            ul_ref[pl.ds(r0, rc), lanes] = (y * g[:, lanes]).astype(BF16)
        return carry

    lax.fori_loop(0, n // rc, fin_lat, 0)


def rglru(xr_l, xr_c, pb, wcat, gate_b, lam, layer, *, nb, n, c):
    tcw = 2 * LRU_BW
    width = xr_l.shape[1]
    rb = (nb * n) // c
    return pl.pallas_call(
        functools.partial(_lru_kernel, n=n, c=c, rc=min(128, n // LRU_SEGS)),
        grid=(nb, width // tcw),
        in_specs=[pl.BlockSpec((n, tcw), lambda b, j: (b, j)),
                  pl.BlockSpec((n, tcw), lambda b, j: (b, j)),
                  pl.BlockSpec((c, tcw), lambda b, j: (b, j)),
                  pl.BlockSpec((c, tcw), lambda b, j: (rb + b, j)),
                  pl.BlockSpec((None, 2, LRU_BW, 4 * LRU_BW), lambda b, j: (layer, j, 0, 0)),
                  pl.BlockSpec((None, 4, tcw), lambda b, j: (layer, 0, j)),
                  pl.BlockSpec((None, 2, tcw), lambda b, j: (layer, 0, j))],
        out_specs=[pl.BlockSpec((n, tcw), lambda b, j: (b, j)),
                   pl.BlockSpec((c, tcw), lambda b, j: (b, j))],
        out_shape=[jax.ShapeDtypeStruct((nb * n, width), BF16),
                   jax.ShapeDtypeStruct((nb * c, width), BF16)],
        scratch_shapes=([pltpu.VMEM((c, tcw), F32)] * 4
                        + [pltpu.VMEM((tcw // LRU_BW, n, LRU_BW), F32)] * 4
                        + [pltpu.VMEM((2 * tcw // LRU_BW, LRU_SEGS, LRU_BW), F32)]),
        compiler_params=_params(2, 56), name="rglru",
    )(xr_l, pb, xr_c, pb, wcat, gate_b, lam)


def _branch_kernel(u0, u1, u2, m0, m1, m2, w_ref, o_ref):
    acc = None
    for k, (u, mr) in enumerate(((u0, m0), (u1, m1), (u2, m2))):
        t = jax.nn.sigmoid(mr[...].astype(F32)) * _dot(u[...], w_ref[k])
        acc = t if acc is None else acc + t
    o_ref[...] = acc.astype(o_ref.dtype)


def branch_merge(u_ret, u_ssd, u_lru, pb, w_branch, layer, *, row_block0, merge_col0, tm, tn):
    m, d = u_ret.shape
    u_spec = pl.BlockSpec((tm, d), lambda j, i: (i, 0))
    m_spec = lambda k: pl.BlockSpec((tm, tn), lambda j, i: (row_block0 + i, (merge_col0 + k * d) // tn + j))
    return pl.pallas_call(
        _branch_kernel,
        grid=(d // tn, m // tm),
        in_specs=[u_spec, u_spec, u_spec, m_spec(0), m_spec(1), m_spec(2),
                  pl.BlockSpec((None, 3, d, tn), lambda j, i: (layer, 0, 0, j))],
        out_specs=pl.BlockSpec((tm, tn), lambda j, i: (i, j)),
        out_shape=jax.ShapeDtypeStruct((m, d), BF16),
        compiler_params=_params(2, 56), name="branch_merge",
    )(u_ret, u_ssd, u_lru, pb, pb, pb, w_branch)


def _ffn_kernel(*refs, gated):
    if gated:
        x_ref, w1_ref, w3_ref, w2_ref, gate_ref, o_ref, acc = refs
    else:
        x_ref, w1_ref, w3_ref, w2_ref, o_ref, acc = refs
    f = pl.program_id(1)

    @pl.when(f == 0)
    def _():
        acc[...] = jnp.zeros_like(acc)

    x = x_ref[...]
    hid = _silu(_dot(x, w1_ref[...].astype(BF16))) * _dot(x, w3_ref[...].astype(BF16))
    if gated:
        hid = hid * gate_ref[...]
    acc[...] += _dot(hid.astype(BF16), w2_ref[...].astype(BF16))

    @pl.when(f == pl.num_programs(1) - 1)
    def _():
        o_ref[...] = acc[...].astype(o_ref.dtype)


def ffn_dense(h, w1, w3, w2, j, *, tm, tf):
    m, d = h.shape
    nf = w1.shape[-1] // tf
    return pl.pallas_call(
        functools.partial(_ffn_kernel, gated=False),
        grid=(m // tm, nf),
        in_specs=[pl.BlockSpec((tm, d), lambda i, f: (i, 0)),
                  pl.BlockSpec((None, d, tf), lambda i, f: (j, 0, f)),
                  pl.BlockSpec((None, d, tf), lambda i, f: (j, 0, f)),
                  pl.BlockSpec((None, tf, d), lambda i, f: (j, f, 0))],
        out_specs=pl.BlockSpec((tm, d), lambda i, f: (i, 0)),
        out_shape=jax.ShapeDtypeStruct((m, d), BF16),
        scratch_shapes=[pltpu.VMEM((tm, d), F32)],
        compiler_params=_params(2, 56), name="ffn_dense",
    )(h, w1, w3, w2)


def _moe_kernel(te_ref, nv_ref, src_ref, dst_ref, wgt_ref, x_hbm, w1_ref, w3_ref, w2_ref, y_hbm,
                xg, xb, acc, sem_in, sem_out, *, tm):
    i = pl.program_id(0)
    f = pl.program_id(1)
    nf = pl.num_programs(1)
    valid = i < nv_ref[0]

    def row_in(r):
        return pltpu.make_async_copy(x_hbm.at[pl.ds(src_ref[0, r], 1), :], xg.at[pl.ds(r, 1), :], sem_in)

    def row_out(r):
        return pltpu.make_async_copy(xg.at[pl.ds(r, 1), :], y_hbm.at[pl.ds(dst_ref[0, r], 1), :], sem_out)

    def for_rows(fn):
        def body(g, carry):
            for k in range(8):
                fn(g * 8 + k, k % 2)
            return carry
        lax.fori_loop(0, tm // 8, body, 0)

    @pl.when(jnp.logical_and(valid, f == 0))
    def _():
        for_rows(lambda r, p: row_in(r).start(priority=p))
        pltpu.make_async_copy(x_hbm.at[pl.ds(0, tm), :], xg, sem_in).wait()
        xb[...] = xg[...].astype(BF16)
        acc[...] = jnp.zeros_like(acc)

    @pl.when(valid)
    def _():
        x = xb[...]
        hid = _silu(_dot(x, w1_ref[...].astype(BF16))) * _dot(x, w3_ref[...].astype(BF16))
        acc[...] += _dot(hid.astype(BF16), w2_ref[...].astype(BF16))

    @pl.when(f == nf - 1)
    def _():
        xg[...] = jnp.where(valid, acc[...] * wgt_ref[...], 0.0)
        for_rows(lambda r, p: row_out(r).start(priority=p))
        pltpu.make_async_copy(xg, y_hbm.at[pl.ds(0, tm), :], sem_out).wait()


def ffn_experts(h, route, w1, w3, w2, j, *, tm, tf):
    m, d = h.shape
    n_exp = w1.shape[1]
    nf = w1.shape[-1] // tf
    n_pairs = 2 * m
    n_tiles = n_pairs // tm + n_exp
    n_slots = n_tiles * tm

    e_flat = route[:, :2].astype(jnp.int32).T.reshape(-1)
    w_flat = route[:, 2:4].T.reshape(-1)
    order = jnp.argsort(e_flat, stable=True).astype(jnp.int32)
    counts = jnp.sum(e_flat[None, :] == jnp.arange(n_exp, dtype=jnp.int32)[:, None], axis=1).astype(jnp.int32)
    starts = jnp.cumsum(counts) - counts
    padded = ((counts + tm - 1) // tm) * tm
    pends = jnp.cumsum(padded)
    pstarts = pends - padded
    n_valid = (pends[-1] // tm).astype(jnp.int32)
    tile_start = jnp.arange(n_tiles, dtype=jnp.int32) * tm
    tile_e_raw = jnp.sum(tile_start[:, None] >= pends[None, :], axis=1).astype(jnp.int32)
    tile_e = jnp.minimum(tile_e_raw, tile_e_raw[jnp.maximum(n_valid - 1, 0)])
    slot_e = jnp.repeat(tile_e_raw, tm)
    slot_ec = jnp.minimum(slot_e, n_exp - 1)
    rank = jnp.arange(n_slots, dtype=jnp.int32) - pstarts[slot_ec]
    real = jnp.logical_and(slot_e < n_exp, rank < counts[slot_ec])
    pair = order[jnp.clip(starts[slot_ec] + rank, 0, n_pairs - 1)]
    src = jnp.where(real, pair % m, 0)
    slot_id = jnp.arange(n_slots, dtype=jnp.int32)
    pad_row = n_pairs + slot_id - (starts + counts)[slot_ec]
    dst = jnp.where(real, pair, jnp.where(slot_e < n_exp, pad_row, slot_id))
    wgt = jnp.where(real, w_flat[pair], 0.0)

    def w_idx(i, f, te, nv):
        return jnp.where(i < nv[0], f, nf - 1)

    grid_spec = pltpu.PrefetchScalarGridSpec(
        num_scalar_prefetch=2,
        grid=(n_tiles, nf),
        in_specs=[pl.BlockSpec((None, 1, tm), lambda i, f, te, nv: (i, 0, 0), memory_space=pltpu.SMEM),
                  pl.BlockSpec((None, 1, tm), lambda i, f, te, nv: (i, 0, 0), memory_space=pltpu.SMEM),
                  pl.BlockSpec((tm, 1), lambda i, f, te, nv: (i, 0)),
                  pl.BlockSpec(memory_space=pl.ANY),
                  pl.BlockSpec((None, None, d, tf), lambda i, f, te, nv: (j, te[i], 0, w_idx(i, f, te, nv))),
                  pl.BlockSpec((None, None, d, tf), lambda i, f, te, nv: (j, te[i], 0, w_idx(i, f, te, nv))),
                  pl.BlockSpec((None, None, tf, d), lambda i, f, te, nv: (j, te[i], w_idx(i, f, te, nv), 0))],
        out_specs=pl.BlockSpec(memory_space=pl.ANY),
        scratch_shapes=[pltpu.VMEM((tm, d), F32), pltpu.VMEM((tm, d), BF16), pltpu.VMEM((tm, d), F32),
                        pltpu.SemaphoreType.DMA(()), pltpu.SemaphoreType.DMA(())],
    )
    return pl.pallas_call(
        functools.partial(_moe_kernel, tm=tm),
        grid_spec=grid_spec,
        out_shape=jax.ShapeDtypeStruct((n_slots, d), F32),
        compiler_params=_params(2, 56), name="ffn_experts",
    )(tile_e, n_valid.reshape(1), src.reshape(n_tiles, 1, tm), dst.reshape(n_tiles, 1, tm),
      wgt.reshape(n_slots, 1), h, w1, w3, w2)


def _router_kernel(h_ref, w_ref, b_ref, o_ref):
    logits = _dot(h_ref[...].astype(BF16), w_ref[...].astype(BF16)) + b_ref[...]
    lane = lax.broadcasted_iota(jnp.int32, logits.shape, 1).astype(F32)
    neg = -jnp.inf
    l1 = jnp.where(lane < N_EXPERTS, logits, neg)
    m1 = jnp.max(l1, axis=-1, keepdims=True)
    i1 = jnp.min(jnp.where(l1 == m1, lane, 1e9), axis=-1, keepdims=True)
    l2 = jnp.where(lane == i1, neg, l1)
    m2 = jnp.max(l2, axis=-1, keepdims=True)
    i2 = jnp.min(jnp.where(l2 == m2, lane, 1e9), axis=-1, keepdims=True)
    e2 = jnp.exp(m2 - m1)
    den = 1.0 + e2
    o_ref[...] = (jnp.where(lane == 0.0, i1, 0.0) + jnp.where(lane == 1.0, i2, 0.0)
                  + jnp.where(lane == 2.0, 1.0 / den, 0.0) + jnp.where(lane == 3.0, e2 / den, 0.0))


def router(h, router_w, router_b, j, *, tm):
    m, d = h.shape
    return pl.pallas_call(
        _router_kernel,
        grid=(m // tm,),
        in_specs=[pl.BlockSpec((tm, d), lambda i: (i, 0)),
                  pl.BlockSpec((None, d, 128), lambda i: (j, 0, 0)),
                  pl.BlockSpec((None, 1, 128), lambda i: (j, 0, 0))],
        out_specs=pl.BlockSpec((tm, 128), lambda i: (i, 0)),
        out_shape=jax.ShapeDtypeStruct((m, 128), F32),
        compiler_params=_params(1, 32), name="router",
    )(h, router_w, router_b)


def kernel(x, c, ctx, c_ctx, w_mod, b_mod, norm1_g, norm2_g, w_in, ret_decay, ret_gn_g, ssd_conv_w, ssd_conv_b, ssd_dt_bias, ssd_a_log, ssd_d, ssd_norm_g, lru_conv_w, lru_conv_b, lru_gate_w, lru_gate_b, lru_lambda, w_branch, w_out, ffn_w1, ffn_w3, ffn_w2, router_w, router_b, moe_w1, moe_w3, moe_w2, final_g):
    nb, n, d = x.shape
    c_len = ctx.shape[1]
    depth = w_mod.shape[0]
    n_lat, n_ctx = nb * n, nb * c_len
    lat_tiles_per_seq = n // 512

    qkvgz_xbc = 11264
    tail0 = qkvgz_xbc + 2 * SSD_HEADS
    n_tail = w_in.shape[-1] - tail0
    w_in_t = jnp.swapaxes(w_in, 1, 2)
    w_br = w_branch.astype(BF16)

    c8 = jnp.concatenate([c, c_ctx[None, :], jnp.zeros((8 - nb - 1, d), F32)], axis=0)
    mods3 = modulation(c8, w_mod, b_mod).reshape(depth * 8 * 6, 1, d)
    seg_lat = lambda i: i // lat_tiles_per_seq
    seg_ctx = lambda i: nb
    log_g = jax.nn.log_sigmoid(ret_decay.astype(F32))
    dtb = jnp.pad(ssd_dt_bias.reshape(depth, 1, 2 * SSD_HEADS), ((0, 0), (0, 0), (0, 128 - 2 * SSD_HEADS)))
    aneg = jnp.pad(-jnp.exp(ssd_a_log.astype(F32)).reshape(depth, 1, 2 * SSD_HEADS),
                   ((0, 0), (0, 0), (0, 128 - 2 * SSD_HEADS)))
    dskip = jnp.repeat(ssd_d, SSD_HEADDIM, axis=-1).reshape(depth, 1, SSD_HEADS * SSD_HEADDIM)
    wcat = jnp.transpose(lru_gate_w, (0, 3, 4, 1, 2, 5)).reshape(depth, 16, LRU_BW, 4 * LRU_BW).astype(BF16)
    gate_b = lru_gate_b.reshape(depth, 4, -1)
    router_wp = jnp.pad(router_w, ((0, 0), (0, 0), (0, 128 - N_EXPERTS)))
    router_bp = jnp.pad(router_b, ((0, 0), (0, 128 - N_EXPERTS)))[:, None, :]

    rows = n // GRID_W
    row = jnp.repeat(jnp.arange(rows, dtype=F32), GRID_W)
    colp = jnp.tile(jnp.arange(GRID_W, dtype=F32), rows)
    n_freq = RET_DK // 4
    inv = ROPE_BASE ** (-jnp.arange(n_freq, dtype=F32) / n_freq)
    ang = jnp.concatenate([row[:, None] * inv, colp[:, None] * inv], axis=-1)
    cos2 = jnp.concatenate([jnp.cos(ang), jnp.cos(ang)], axis=-1)
    sin2 = jnp.concatenate([-jnp.sin(ang), jnp.sin(ang)], axis=-1)

    xl = x.reshape(n_lat, d)
    xc = ctx.reshape(n_ctx, d)
    yl = yc = None
    ctx_rb = n_lat // c_len

    for i in range(depth):
        last = i == depth - 1
        prev_gate = (i - 1, 5)
        g1 = norm1_g[i][None, :]
        xl, hl = resnorm(xl, yl, mods3, g1, seg_fn=seg_lat, gate=prev_gate, sc=(i, 1), sh=(i, 0), out_dtype=BF16, tm=512)
        xc, hc = resnorm(xc, yc, mods3, g1, seg_fn=seg_ctx, gate=prev_gate, sc=(i, 1), sh=(i, 0), out_dtype=BF16, tm=512)
        h = jnp.concatenate([hl, hc], axis=0)
        tm_all = h.shape[0] // 8

        pa = matmul_nt(h, w_in_t, i, row0=0, n_cols=qkvgz_xbc, tn=1024, tm=tm_all, out_dtype=BF16)
        pb = matmul_nt(h, w_in_t, i, row0=tail0, n_cols=n_tail, tn=1024, tm=tm_all, out_dtype=BF16)
        pdt = matmul_nt(h, w_in_t, i, row0=qkvgz_xbc, n_cols=128, tn=128, tm=tm_all, out_dtype=F32)

        xbc_l = dwconv(pa, ssd_conv_w, ssd_conv_b, i, n=n, nb=nb, row_block0=0, col0=8192, n_cols=3072, act=True, out_dtype=BF16)
        xbc_c = dwconv(pa, ssd_conv_w, ssd_conv_b, i, n=c_len, nb=nb, row_block0=ctx_rb, col0=8192, n_cols=3072, act=True, out_dtype=BF16)
        xr_l = dwconv(pb, lru_conv_w, lru_conv_b, i, n=n, nb=nb, row_block0=0, col0=2048, n_cols=2048, act=False, out_dtype=F32)
        xr_c = dwconv(pb, lru_conv_w, lru_conv_b, i, n=c_len, nb=nb, row_block0=ctx_rb, col0=2048, n_cols=2048, act=False, out_dtype=F32)

        ret_l, ret_c = retention(pa, cos2, sin2, log_g[i], ret_gn_g, i, nb=nb, n=n, c=c_len)
        ssd_l, ssd_c = ssd(xbc_l, xbc_c, pdt, pa, dtb, aneg, dskip, i, nb=nb, n=n, c=c_len)
        lru_l, lru_c = rglru(xr_l, xr_c, pb, wcat, gate_b, lru_lambda, i, nb=nb, n=n, c=c_len)
        sg = ssd_norm_g[i][None, :]

        dense = i % 2 == 0
        h2_dtype = BF16 if dense else F32

        def mixer_out(xres, u_ret, y_ssd, u_lru, seg_fn, row_block0, tm):
            _, u_ssd = resnorm(y_ssd, None, mods3, sg, seg_fn=seg_fn, gate=None, sc=None, sh=None, out_dtype=BF16, tm=512)
            merged = branch_merge(u_ret, u_ssd, u_lru, pb, w_br, i, row_block0=row_block0, merge_col0=4096, tm=tm, tn=512)
            y = matmul(merged, w_out, (i,), col_block0=0, n_cols=d, tn=1024, tm=tm, out_dtype=BF16)
            return resnorm(xres, y, mods3, norm2_g[i][None, :], seg_fn=seg_fn, gate=(i, 2), sc=(i, 4), sh=(i, 3),
                           out_dtype=h2_dtype, tm=512)

        xl, h2l = mixer_out(xl, ret_l, ssd_l, lru_l, seg_lat, 0, 1024)
        if not last:
            xc, h2c = mixer_out(xc, ret_c, ssd_c, lru_c, seg_ctx, n_lat // 512, 512)
        j = i // 2
        if dense:
            yl = ffn_dense(h2l, ffn_w1, ffn_w3, ffn_w2, j, tm=1024, tf=512)
            if not last:
                yc = ffn_dense(h2c, ffn_w1, ffn_w3, ffn_w2, j, tm=512, tf=512)
        else:
            h2 = h2l if last else jnp.concatenate([h2l, h2c], axis=0)
            route = router(h2, router_wp, router_bp, j, tm=512)
            y2 = ffn_experts(h2, route, moe_w1, moe_w3, moe_w2, j, tm=512, tf=512)
            mb = h2.shape[0] // 512
            yl = [(y2, 0), (y2, mb)]
            yc = [(y2, n_lat // 512), (y2, mb + n_lat // 512)]

    _, out = resnorm(xl, yl, mods3, final_g[None, :], seg_fn=seg_lat, gate=(depth - 1, 5), sc=None, sh=None, out_dtype=F32, tm=512)
    return out.reshape(nb, n, d)
```
